```python
import math
import jax, jax.numpy as jnp
from jax import lax
import numpy as np

D_MODEL = 2048
BATCH = 4
SEQ = 2048
DEPTH = 4
DEC_BATCH = 128
DEC_SEQ = 8
PAST_LEN = 16384
PAGE_SIZE = 128

D_LRU = D_MODEL // 2
LRU_BLOCKS = 16
LRU_BLOCK = D_LRU // LRU_BLOCKS
LRU_CONV = 4
LRU_C = 8.0
D_RW = D_MODEL // 2
RW_HEAD = 64
RW_HEADS = D_RW // RW_HEAD
LORA_W = max(32, int(round(math.sqrt(D_RW) * 1.8 / 32)) * 32)
LORA_A = max(32, int(round(math.sqrt(D_RW) * 1.8 / 32)) * 32)
LORA_G = max(32, int(round(D_RW ** 0.8 * 0.6 / 32)) * 32)
N_RW_PROJ = 3 * D_RW + LORA_W + LORA_A + LORA_G
N_IN = 2 * D_LRU + N_RW_PROJ + 2 * D_MODEL
D_FF = 5632
FFN_CONV = 3
RMS_EPS = 1e-6
GN_EPS = 64e-5
N_MOD = 6

kernel_name = "hybrid_rglru_rwkv7_adaln_convffn_step"


def rms_norm(x, g):
    xf = x.astype(jnp.float32)
    y = xf * lax.rsqrt(jnp.mean(xf * xf, axis=-1, keepdims=True) + RMS_EPS)
    return (y * g.astype(jnp.float32)).astype(x.dtype)


def causal_dwconv(x, buf, w, b):
    width = w.shape[0]
    t_len = x.shape[1]
    xx = jnp.concatenate([buf.astype(x.dtype), x], axis=1)
    out = b
    for j in range(width):
        out = out + w[j] * xx[:, j:j + t_len]
    return out, xx[:, xx.shape[1] - (width - 1):]


def token_shift(p, buf):
    prev = jnp.concatenate([buf[:, None].astype(p.dtype), p[:, :-1]], axis=1)
    return prev, p[:, -1]


def rg_lru(x, h0, w_a, b_a, w_i, b_i, lam):
    bsz, t_len, _ = x.shape
    xb = x.reshape(bsz, t_len, LRU_BLOCKS, LRU_BLOCK)
    r = jax.nn.sigmoid(jnp.einsum('btnd,nde->btne', xb, w_a).reshape(bsz, t_len, D_LRU) + b_a)
    i = jax.nn.sigmoid(jnp.einsum('btnd,nde->btne', xb, w_i).reshape(bsz, t_len, D_LRU) + b_i)
    log_a = -LRU_C * r.astype(jnp.float32) * jax.nn.softplus(-lam.astype(jnp.float32))
    a = jnp.exp(log_a)
    inp = jnp.sqrt(-jnp.expm1(2.0 * log_a)) * (i * x).astype(jnp.float32)
    inp = inp.at[:, 0].add(a[:, 0] * h0.astype(jnp.float32))

    def combine(left, right):
        a_l, b_l = left
        a_r, b_r = right
        return a_l * a_r, a_r * b_l + b_r

    _, h = lax.associative_scan(combine, (a, inp), axis=1)
    return h.astype(x.dtype), h[:, -1].astype(h0.dtype)


def rwkv7_recurrence(r, w, k, v, kk, a, s0):
    def step(s, inp):
        r_t, w_t, k_t, v_t, kk_t, a_t = inp
        s_kk = jnp.einsum('bhvk,bhk->bhv', s, kk_t)
        s = (s * w_t[:, :, None, :] - s_kk[..., None] * (kk_t * a_t)[:, :, None, :]
             + v_t[..., None] * k_t[:, :, None, :])
        y = jnp.einsum('bhvk,bhk->bhv', s, r_t)
        return s, y

    xs = (jnp.moveaxis(r, 1, 0), jnp.moveaxis(w, 1, 0), jnp.moveaxis(k, 1, 0),
          jnp.moveaxis(v, 1, 0), jnp.moveaxis(kk, 1, 0), jnp.moveaxis(a, 1, 0))
    s_fin, ys = lax.scan(step, s0.astype(jnp.float32), xs)
    return jnp.moveaxis(ys, 0, 1), s_fin.astype(s0.dtype)


def layer(x, c, st, l, P):
    lru_buf, lru_h0, shift_buf, s0, ffn_buf = st
    bsz, t_len, _ = x.shape
    mod = jnp.dot(jax.nn.silu(c), P['w_ada'][l]) + P['b_ada'][l]
    sh1, sc1, gt1, sh2, sc2, gt2 = jnp.split(mod[:, None, :], N_MOD, axis=-1)

    h = rms_norm(x, P['norm_mix'][l]) * (1.0 + sc1) + sh1
    p = jnp.dot(h, P['w_in'][l])
    lru_x, lru_gate, rwp, gates = jnp.split(
        p, [D_LRU, 2 * D_LRU, 2 * D_LRU + N_RW_PROJ], axis=-1)

    xc, new_lru_buf = causal_dwconv(lru_x, lru_buf, P['lru_conv_w'][l], P['lru_conv_b'][l])
    hl, new_lru_h = rg_lru(xc, lru_h0, P['lru_wa'][l], P['lru_ba'][l],
                           P['lru_wi'][l], P['lru_bi'][l], P['lru_lambda'][l])
    y_a = jnp.dot(jax.nn.gelu(lru_gate) * hl, P['w_pa'][l])

    prev, new_shift = token_shift(rwp, shift_buf)
    xs = rwp + (prev - rwp) * P['rw_mu'][l]
    r_, k_, v_, lw, la, lg = jnp.split(
        xs, [D_RW, 2 * D_RW, 3 * D_RW, 3 * D_RW + LORA_W, 3 * D_RW + LORA_W + LORA_A], axis=-1)
    w_log = -jax.nn.softplus(-(P['rw_w0'][l] + jnp.dot(jnp.tanh(lw), P['rw_w2'][l])).astype(jnp.float32)) - 0.5
    decay = jnp.exp(-jnp.exp(w_log))
    a_icl = jax.nn.sigmoid(P['rw_a0'][l] + jnp.dot(la, P['rw_a2'][l]))
    gg = jnp.dot(jax.nn.sigmoid(lg), P['rw_g2'][l])
    heads = lambda t: t.astype(jnp.float32).reshape(bsz, t_len, RW_HEADS, RW_HEAD)
    kk = heads(k_ * P['rw_kk'][l])
    kk = kk / jnp.maximum(jnp.sqrt(jnp.sum(kk * kk, axis=-1, keepdims=True)), 1e-12)
    k_mod = heads(k_ * (1.0 + (a_icl - 1.0) * P['rw_ka'][l]))
    r_h, v_h, a_h, w_h = heads(r_), heads(v_), heads(a_icl), heads(decay)
    y_rw, new_s = rwkv7_recurrence(r_h, w_h, k_mod, v_h, kk, a_h, s0)
    mu = jnp.mean(y_rw, axis=-1, keepdims=True)
    var = jnp.mean(jnp.square(y_rw - mu), axis=-1, keepdims=True)
    gn = ((y_rw - mu) * lax.rsqrt(var + GN_EPS) * P['rw_gn_g'][l].astype(jnp.float32).reshape(RW_HEADS, RW_HEAD)
          + P['rw_gn_b'][l].astype(jnp.float32).reshape(RW_HEADS, RW_HEAD))
    bonus = jnp.sum(r_h * k_mod * P['rw_rk'][l].astype(jnp.float32), axis=-1, keepdims=True) * v_h
    o_rw = (gn + bonus).reshape(bsz, t_len, D_RW).astype(x.dtype) * gg
    y_b = jnp.dot(o_rw, P['w_pb'][l])

    g_a, g_b = jnp.split(gates, 2, axis=-1)
    merged = jax.nn.sigmoid(g_a) * y_a + jax.nn.sigmoid(g_b) * y_b
    x = x + gt1 * jnp.dot(merged, P['w_o'][l])

    h2 = rms_norm(x, P['norm_ffn'][l]) * (1.0 + sc2) + sh2
    u = jnp.dot(h2, P['w_up'][l])
    uc, new_ffn_buf = causal_dwconv(u, ffn_buf, P['ffn_conv_w'][l], P['ffn_conv_b'][l])
    u_g, u_v = jnp.split(uc, 2, axis=-1)
    x = x + gt2 * jnp.dot(jax.nn.silu(u_g) * u_v, P['w_down'][l])
    return x, (new_lru_buf, new_lru_h, new_shift, new_s, new_ffn_buf)


def trunk(x, c, states, P):
    bsz = x.shape[0]
    if states is None:
        states = (jnp.zeros((DEPTH, bsz, LRU_CONV - 1, D_LRU), x.dtype),
                  jnp.zeros((DEPTH, bsz, D_LRU), x.dtype),
                  jnp.zeros((DEPTH, bsz, N_RW_PROJ), x.dtype),
                  jnp.zeros((DEPTH, bsz, RW_HEADS, RW_HEAD, RW_HEAD), x.dtype),
                  jnp.zeros((DEPTH, bsz, FFN_CONV - 1, 2 * D_FF), x.dtype))
    outs = ([], [], [], [], [])
    for l in range(DEPTH):
        st = tuple(s[l] for s in states)
        x, new_st = layer(x, c, st, l, P)
        for lst, ns in zip(outs, new_st):
            lst.append(ns)
    y = rms_norm(x, P['norm_final'])
    return y, tuple(jnp.stack(lst, axis=0) for lst in outs)


def setup_inputs(seed: int = 0) -> dict:
    key = jax.random.key(seed)
    ks = iter(jax.random.split(key, 64))
    f32 = jnp.float32
    nrm = lambda shape, scale: jax.random.normal(next(ks), shape, f32) * scale
    uni = lambda shape, lo, hi: jax.random.uniform(next(ks), shape, f32, lo, hi)
    u_lam = uni((DEPTH, D_LRU), 0.9, 0.999) ** (1.0 / LRU_C)
    return {
        'x_prompt': nrm((BATCH, SEQ, D_MODEL), 1.0),
        'x_sample': nrm((DEC_BATCH, DEC_SEQ, D_MODEL), 1.0),
        'c_prompt': nrm((BATCH, D_MODEL), 1.0),
        'c_sample': nrm((DEC_BATCH, D_MODEL), 1.0),
        'state_lru_conv': nrm((DEPTH, DEC_BATCH, LRU_CONV - 1, D_LRU), 1.0),
        'state_lru_h': nrm((DEPTH, DEC_BATCH, D_LRU), 0.5),
        'state_rwkv_shift': nrm((DEPTH, DEC_BATCH, N_RW_PROJ), 1.0),
        'state_rwkv_S': nrm((DEPTH, DEC_BATCH, RW_HEADS, RW_HEAD, RW_HEAD), 0.5),
        'state_ffn_conv': nrm((DEPTH, DEC_BATCH, FFN_CONV - 1, 2 * D_FF), 1.0),
        'w_ada': nrm((DEPTH, D_MODEL, N_MOD * D_MODEL), 0.5 * D_MODEL ** -0.5),
        'b_ada': nrm((DEPTH, N_MOD * D_MODEL), 0.01),
        'norm_mix': 1.0 + nrm((DEPTH, D_MODEL), 0.02),
        'norm_ffn': 1.0 + nrm((DEPTH, D_MODEL), 0.02),
        'w_in': nrm((DEPTH, D_MODEL, N_IN), D_MODEL ** -0.5),
        'lru_conv_w': nrm((DEPTH, LRU_CONV, D_LRU), LRU_CONV ** -0.5),
        'lru_conv_b': nrm((DEPTH, D_LRU), 0.01),
        'lru_wa': nrm((DEPTH, LRU_BLOCKS, LRU_BLOCK, LRU_BLOCK), LRU_BLOCK ** -0.5),
        'lru_ba': nrm((DEPTH, D_LRU), 0.01),
        'lru_wi': nrm((DEPTH, LRU_BLOCKS, LRU_BLOCK, LRU_BLOCK), LRU_BLOCK ** -0.5),
        'lru_bi': nrm((DEPTH, D_LRU), 0.01),
        'lru_lambda': jnp.log(u_lam) - jnp.log1p(-u_lam),
        'w_pa': nrm((DEPTH, D_LRU, D_MODEL), D_LRU ** -0.5),
        'rw_mu': uni((DEPTH, N_RW_PROJ), 0.0, 1.0),
        'rw_w0': uni((DEPTH, D_RW), -6.0, -1.0),
        'rw_w2': nrm((DEPTH, LORA_W, D_RW), 0.5 * LORA_W ** -0.5),
        'rw_a0': nrm((DEPTH, D_RW), 0.1),
        'rw_a2': nrm((DEPTH, LORA_A, D_RW), LORA_A ** -0.5),
        'rw_g2': nrm((DEPTH, LORA_G, D_RW), LORA_G ** -0.5),
        'rw_kk': 0.85 + nrm((DEPTH, D_RW), 0.02),
        'rw_ka': 1.0 + nrm((DEPTH, D_RW), 0.02),
        'rw_rk': nrm((DEPTH, RW_HEADS, RW_HEAD), 0.1),
        'rw_gn_g': 1.0 + nrm((DEPTH, D_RW), 0.02),
        'rw_gn_b': nrm((DEPTH, D_RW), 0.01),
        'w_pb': nrm((DEPTH, D_RW, D_MODEL), D_RW ** -0.5),
        'w_o': nrm((DEPTH, D_MODEL, D_MODEL), D_MODEL ** -0.5),
        'w_up': nrm((DEPTH, D_MODEL, 2 * D_FF), D_MODEL ** -0.5),
        'ffn_conv_w': nrm((DEPTH, FFN_CONV, 2 * D_FF), FFN_CONV ** -0.5),
        'ffn_conv_b': nrm((DEPTH, 2 * D_FF), 0.01),
        'w_down': nrm((DEPTH, D_FF, D_MODEL), D_FF ** -0.5),
        'norm_final': 1.0 + nrm((D_MODEL,), 0.02),
    }


def reference(x_prompt, x_sample, c_prompt, c_sample, state_lru_conv, state_lru_h,
              state_rwkv_shift, state_rwkv_S, state_ffn_conv, w_ada, b_ada, norm_mix, norm_ffn,
              w_in, lru_conv_w, lru_conv_b, lru_wa, lru_ba, lru_wi, lru_bi, lru_lambda, w_pa,
              rw_mu, rw_w0, rw_w2, rw_a0, rw_a2, rw_g2, rw_kk, rw_ka, rw_rk, rw_gn_g, rw_gn_b,
              w_pb, w_o, w_up, ffn_conv_w, ffn_conv_b, w_down, norm_final):
    P = dict(w_ada=w_ada, b_ada=b_ada, norm_mix=norm_mix, norm_ffn=norm_ffn, w_in=w_in,
             lru_conv_w=lru_conv_w, lru_conv_b=lru_conv_b, lru_wa=lru_wa, lru_ba=lru_ba,
             lru_wi=lru_wi, lru_bi=lru_bi, lru_lambda=lru_lambda, w_pa=w_pa, rw_mu=rw_mu,
             rw_w0=rw_w0, rw_w2=rw_w2, rw_a0=rw_a0, rw_a2=rw_a2, rw_g2=rw_g2, rw_kk=rw_kk,
             rw_ka=rw_ka, rw_rk=rw_rk, rw_gn_g=rw_gn_g, rw_gn_b=rw_gn_b, w_pb=w_pb, w_o=w_o,
             w_up=w_up, ffn_conv_w=ffn_conv_w, ffn_conv_b=ffn_conv_b, w_down=w_down,
             norm_final=norm_final)
    y_prompt, (p_lru_conv, p_lru_h, p_rwkv_shift, p_rwkv_S, p_ffn_conv) = trunk(
        x_prompt, c_prompt, None, P)
    y_sample, (s_lru_conv, s_lru_h, s_rwkv_shift, s_rwkv_S, s_ffn_conv) = trunk(
        x_sample, c_sample,
        (state_lru_conv, state_lru_h, state_rwkv_shift, state_rwkv_S, state_ffn_conv), P)
    return (y_prompt, y_sample, p_lru_conv, p_lru_h, p_rwkv_shift, p_rwkv_S, p_ffn_conv,
            s_lru_conv, s_lru_h, s_rwkv_shift, s_rwkv_S, s_ffn_conv)
```

```python
import functools
import math

import jax
import jax.numpy as jnp
from jax import lax
from jax.experimental import pallas as pl
from jax.experimental.pallas import tpu as pltpu

F32 = jnp.float32
BF16 = jnp.bfloat16

LANES = 128
SUBLANES = 8
RW_HEAD = 64
ROWS = 2 * RW_HEAD
LRU_C = 8.0
RMS_EPS = 1e-6
GN_EPS = 64e-5
VMEM_LIMIT = 56 * 1024 * 1024


def _cdiv(a, b):
    return -(-a // b)


def _dot(a, b):
    return jnp.dot(a.astype(BF16), b.astype(BF16), preferred_element_type=F32)


def _dot_nt(a, b):
    return lax.dot_general(a.astype(BF16), b.astype(BF16), (((1,), (1,)), ((), ())),
                           preferred_element_type=F32)


def _dot_tn(a, b):
    return lax.dot_general(a.astype(BF16), b.astype(BF16), (((0,), (0,)), ((), ())),
                           preferred_element_type=F32)


def _dot_hi(a, b):
    a_hi, a_lo = _split(a, 2)
    b_hi, b_lo = _split(b, 2)
    d = lambda p, q: jnp.dot(p, q, preferred_element_type=F32)
    return d(a_hi, b_hi) + (d(a_hi, b_lo) + d(a_lo, b_hi))


def _split(x, n):
    parts = []
    for _ in range(n - 1):
        p = x.astype(BF16)
        parts.append(p)
        x = x - p.astype(F32)
    parts.append(x.astype(BF16))
    return parts


def _dot_exact_rhs(x, m, n=3):
    return sum(jnp.dot(p, m, preferred_element_type=F32) for p in _split(x, n))


def _dot_exact_lhs(m, x, n=3):
    return sum(jnp.dot(m, p, preferred_element_type=F32) for p in _split(x, n))


def _sigmoid(x):
    return 1.0 / (1.0 + jnp.exp(-x))


def _softplus(x):
    return jnp.maximum(x, 0.0) + jnp.log1p(jnp.exp(-jnp.abs(x)))


def _silu(x):
    return x * _sigmoid(x)


def _gelu_tanh(x):
    return 0.5 * x * (1.0 + jnp.tanh(math.sqrt(2.0 / math.pi) * (x + 0.044715 * (x * x * x))))


def _mm_kernel(*refs, act, bias):
    if bias:
        a_ref, w_ref, b_ref, o_ref, wbf_ref = refs
    else:
        a_ref, w_ref, o_ref, wbf_ref = refs

    @pl.when(pl.program_id(1) == 0)
    def _cast_weights():
        wbf_ref[...] = w_ref[...].astype(BF16)

    a = a_ref[...]
    if act == "silu":
        a = _silu(a.astype(F32))
    acc = jnp.dot(a.astype(BF16), wbf_ref[...], preferred_element_type=F32)
    if bias:
        acc = acc + b_ref[...]
    o_ref[...] = acc.astype(o_ref.dtype)


def _pick_tile(n, target):
    assert n % LANES == 0
    units = n // LANES
    best = max(u for u in range(1, units + 1) if units % u == 0 and u * LANES <= max(target, LANES))
    return best * LANES


def _matmul(a, w, layer, *, tm, tn, n=None, bias=None, act=None, out_dtype=F32):
    m, k = a.shape
    n = w.shape[-1] if n is None else n
    tm = min(tm, m)
    tn = _pick_tile(n, tn)
    assert m % tm == 0
    grid = (_cdiv(n, tn), _cdiv(m, tm))
    in_specs = [pl.BlockSpec((tm, k), lambda j, i: (i, 0)),
                pl.BlockSpec((None, k, tn), lambda j, i: (layer, 0, j))]
    args = [a, w]
    if bias is not None:
        in_specs.append(pl.BlockSpec((None, 1, tn), lambda j, i: (layer, 0, j)))
        args.append(bias)
    return pl.pallas_call(
        functools.partial(_mm_kernel, act=act, bias=bias is not None),
        grid=grid,
        in_specs=in_specs,
        out_specs=pl.BlockSpec((tm, tn), lambda j, i: (i, j)),
        out_shape=jax.ShapeDtypeStruct((m, n), out_dtype),
        scratch_shapes=[pltpu.VMEM((k, tn), BF16)],
        compiler_params=pltpu.CompilerParams(
            dimension_semantics=("arbitrary", "arbitrary"), vmem_limit_bytes=VMEM_LIMIT),
    )(*args)


def _norm_mod_kernel(x_ref, g_ref, sc_ref, sh_ref, o_ref):
    x = x_ref[...]
    y = x * lax.rsqrt(jnp.mean(x * x, axis=-1, keepdims=True) + RMS_EPS) * g_ref[...]
    o_ref[...] = (y * (1.0 + sc_ref[...]) + sh_ref[...]).astype(o_ref.dtype)


def _norm_mod(x, g, layer, mod3, sc_idx, sh_idx, *, bb, tt):
    b, t, d = x.shape
    return pl.pallas_call(
        _norm_mod_kernel,
        grid=(b // bb, t // tt),
        in_specs=[pl.BlockSpec((bb, tt, d), lambda i, j: (i, j, 0)),
                  pl.BlockSpec((None, 1, d), lambda i, j: (layer, 0, 0)),
                  pl.BlockSpec((bb, 1, d), lambda i, j: (i, 0, sc_idx)),
                  pl.BlockSpec((bb, 1, d), lambda i, j: (i, 0, sh_idx))],
        out_specs=pl.BlockSpec((bb, tt, d), lambda i, j: (i, j, 0)),
        out_shape=jax.ShapeDtypeStruct((b, t, d), BF16),
        compiler_params=pltpu.CompilerParams(
            dimension_semantics=("arbitrary", "arbitrary"), vmem_limit_bytes=VMEM_LIMIT),
    )(x, g, mod3, mod3)


def _final_norm_kernel(x_ref, g_ref, o_ref):
    x = x_ref[...]
    o_ref[...] = x * lax.rsqrt(jnp.mean(x * x, axis=-1, keepdims=True) + RMS_EPS) * g_ref[...]


def _final_norm(x, g, *, bb, tt):
    b, t, d = x.shape
    return pl.pallas_call(
        _final_norm_kernel,
        grid=(b // bb, t // tt),
        in_specs=[pl.BlockSpec((bb, tt, d), lambda i, j: (i, j, 0)),
                  pl.BlockSpec((1, d), lambda i, j: (0, 0))],
        out_specs=pl.BlockSpec((bb, tt, d), lambda i, j: (i, j, 0)),
        out_shape=jax.ShapeDtypeStruct((b, t, d), F32),
        compiler_params=pltpu.CompilerParams(
            dimension_semantics=("arbitrary", "arbitrary"), vmem_limit_bytes=VMEM_LIMIT),
    )(x, g)


def _lru_kernel(x_ref, gate_ref, buf_ref, h0_ref, cw_ref, cb_ref, wa_ref, ba_ref, wi_ref, bi_ref,
                lam_ref, ga_ref, nbuf_ref, nh_ref, xx_ref, hc_ref, a_ref, b_ref, h_ref, *, bb, tt, width):
    c = x_ref.shape[-1]
    rows = bb * tt
    halo = width - 1
    base = SUBLANES - halo

    @pl.when(pl.program_id(1) == 0)
    def _init():
        xx_ref[:, base:SUBLANES, :] = buf_ref[...]
        hc_ref[...] = h0_ref[...]

    xx_ref[:, SUBLANES:SUBLANES + tt, :] = x_ref[...]
    conv = cb_ref[...] + cw_ref[0:1, :] * xx_ref[:, base:base + tt, :]
    for j in range(1, width):
        conv = conv + cw_ref[j:j + 1, :] * xx_ref[:, base + j:base + j + tt, :]
    last = xx_ref[:, tt + base:tt + SUBLANES, :]
    nbuf_ref[...] = last
    xx_ref[:, base:SUBLANES, :] = last

    xc = conv.reshape(rows, c)
    r = _sigmoid(_dot(xc, wa_ref[...]) + ba_ref[...])
    i = _sigmoid(_dot(xc, wi_ref[...]) + bi_ref[...])
    log_a = -LRU_C * r * _softplus(-lam_ref[...])
    a = jnp.exp(log_a)
    inp = jnp.sqrt(-jnp.tanh(log_a) * (a * a + 1.0)) * (i * xc)

    pos = lax.broadcasted_iota(jnp.int32, (rows, 1), 0) % SUBLANES
    for s in (1, 2, 4):
        a_sh = pltpu.roll(a, s, 0)
        b_sh = pltpu.roll(inp, s, 0)
        m = pos >= s
        inp = jnp.where(m, a * b_sh + inp, inp)
        a = jnp.where(m, a * a_sh, a)

    if tt == SUBLANES:
        h = (inp.reshape(bb, tt, c) + a.reshape(bb, tt, c) * hc_ref[...]).reshape(rows, c)
    else:
        a_ref[...] = a
        b_ref[...] = inp

        def group(g, carry):
            sl = pl.ds(pl.multiple_of(g * SUBLANES, SUBLANES), SUBLANES)
            hg = b_ref[sl, :] + a_ref[sl, :] * carry
            h_ref[sl, :] = hg
            return hg[SUBLANES - 1:SUBLANES, :]

        lax.fori_loop(0, rows // SUBLANES, group, hc_ref[0])
        h = h_ref[...]

    h3 = h.reshape(bb, tt, c)
    hc_ref[...] = h3[:, tt - 1:tt, :]
    nh_ref[...] = h3[:, tt - 1:tt, :]
    ga_ref[...] = (_gelu_tanh(gate_ref[...]) * h3).astype(ga_ref.dtype)


def _lru_branch(p_lru, buf, h0, layer, P, *, bb, tt):
    b, t, c2 = p_lru.shape
    c = c2 // 2
    width = P["lru_conv_w"].shape[1]
    assert t >= width - 1 and (tt == SUBLANES or bb == 1)
    lw = lambda shape: pl.BlockSpec((None,) + shape, lambda i, j: (layer,) + (0,) * len(shape))
    kern = functools.partial(_lru_kernel, bb=bb, tt=tt, width=width)
    return pl.pallas_call(
        kern,
        grid=(b // bb, t // tt),
        in_specs=[pl.BlockSpec((bb, tt, c), lambda i, j: (i, j, 0)),
                  pl.BlockSpec((bb, tt, c), lambda i, j: (i, j, 1)),
                  pl.BlockSpec((bb, width - 1, c), lambda i, j: (i, 0, 0)),
                  pl.BlockSpec((bb, 1, c), lambda i, j: (i, 0, 0)),
                  lw((width, c)), lw((1, c)), lw((c, c)), lw((1, c)), lw((c, c)), lw((1, c)), lw((1, c))],
        out_specs=[pl.BlockSpec((bb, tt, c), lambda i, j: (i, j, 0)),
                   pl.BlockSpec((bb, width - 1, c), lambda i, j: (i, 0, 0)),
                   pl.BlockSpec((bb, 1, c), lambda i, j: (i, 0, 0))],
        out_shape=[jax.ShapeDtypeStruct((b, t, c), BF16),
                   jax.ShapeDtypeStruct((b, width - 1, c), F32),
                   jax.ShapeDtypeStruct((b, 1, c), F32)],
        scratch_shapes=[pltpu.VMEM((bb, tt + SUBLANES, c), F32),
                        pltpu.VMEM((bb, 1, c), F32),
                        pltpu.VMEM((bb * tt, c), F32),
                        pltpu.VMEM((bb * tt, c), F32),
                        pltpu.VMEM((bb * tt, c), F32)],
        compiler_params=pltpu.CompilerParams(
            dimension_semantics=("arbitrary", "arbitrary"), vmem_limit_bytes=VMEM_LIMIT),
    )(p_lru, p_lru, buf, h0, P["lru_conv_w"], P["lru_conv_b"], P["lru_wa_bd"], P["lru_ba"],
      P["lru_wi_bd"], P["lru_bi"], P["lru_lambda"])


def _rwkv_kernel(x_ref, sbuf_ref, s0_ref, mu_ref, w0_ref, w2_ref, a0_ref, a2_ref, g2_ref, kkp_ref,
                 ka_ref, rk_ref, gng_ref, gnb_ref, seg_ref, segt_ref,
                 o_ref, nshift_ref, ns_ref,
                 sbd_ref, prev_ref, r_ref, kk_ref, km_ref, b_ref, v_ref, lw_ref, y_ref, tmp_ref,
                 *, bb, tt, chunk, dr):
    rows = bb * tt
    nrw = x_ref.shape[-1]
    pairs = dr // LANES
    gb = RW_HEAD // chunk
    ti = pl.program_id(1)

    @pl.when(ti == 0)
    def _init():
        prev_ref[...] = sbuf_ref[...]
        tmp_ref[...] = jnp.zeros_like(tmp_ref)

        def pack(bi, carry):
            for j in range(pairs):
                sbd_ref[bi, j, 0:RW_HEAD, 0:RW_HEAD] = s0_ref[bi, 2 * j]
                sbd_ref[bi, j, 0:RW_HEAD, RW_HEAD:LANES] = jnp.zeros((RW_HEAD, RW_HEAD), F32)
                tmp_ref[:, 0:RW_HEAD] = s0_ref[bi, 2 * j + 1]
                sbd_ref[bi, j, RW_HEAD:ROWS, :] = pltpu.roll(tmp_ref[...], RW_HEAD, 1)
            return carry

        lax.fori_loop(0, bb, pack, 0)

    x3 = x_ref[...]
    rolled = pltpu.roll(x3.reshape(rows, nrw), 1, 0).reshape(bb, tt, nrw)
    t_pos = lax.broadcasted_iota(jnp.int32, (bb, tt, 1), 1)
    prev3 = jnp.where(t_pos == 0, prev_ref[...], rolled)
    last = x3[:, tt - 1:tt, :]
    prev_ref[...] = last
    nshift_ref[...] = last
    xs = (x3 + (prev3 - x3) * mu_ref[...]).reshape(rows, nrw)

    r = xs[:, 0:dr]
    k = xs[:, dr:2 * dr]
    v = xs[:, 2 * dr:3 * dr]
    lora = xs[:, 3 * dr:]
    w_log = -_softplus(-(w0_ref[...] + _dot(jnp.tanh(lora), w2_ref[...]))) - 0.5
    a = _sigmoid(a0_ref[...] + _dot(lora, a2_ref[...]))
    gg = _dot(_sigmoid(lora), g2_ref[...])

    seg = seg_ref[...]
    segt = segt_ref[...]
    head_sum = lambda z: _dot_exact_rhs(z, seg, 3)
    head_bcast = lambda z: _dot_exact_rhs(z, segt, 2)

    kk = k * kkp_ref[...]
    kk = kk * head_bcast(1.0 / jnp.maximum(jnp.sqrt(head_sum(kk * kk)), 1e-12))
    km = k * (1.0 + (a - 1.0) * ka_ref[...])
    r_ref[...] = r
    kk_ref[...] = kk
    km_ref[...] = km
    b_ref[...] = kk * a
    v_ref[...] = v
    lw_ref[...] = -jnp.exp(w_log)

    ri = lax.broadcasted_iota(jnp.int32, (ROWS, ROWS), 0)
    ci = lax.broadcasted_iota(jnp.int32, (ROWS, ROWS), 1)
    same = (ri // chunk) == (ci // chunk)
    m_strict = same & ((ci % chunk) < (ri % chunk))
    m_incl = same & ((ci % chunk) <= (ri % chunk))
    m_pair = (ri // RW_HEAD) == (ci // RW_HEAD)
    eye = jnp.where(ri == ci, 1.0, 0.0)
    lvl_masks = []
    s = 1
    while s < chunk:
        lvl_masks.append(((ri // (2 * s)) == (ci // (2 * s))) & ((ri // s) != (ci // s)))
        s *= 2
    r64 = lax.broadcasted_iota(jnp.int32, (RW_HEAD, RW_HEAD), 0)
    c64 = lax.broadcasted_iota(jnp.int32, (RW_HEAD, RW_HEAD), 1)
    tril = jnp.where(((r64 // chunk) == (c64 // chunk)) & (c64 <= r64), 1.0, 0.0).astype(BF16)
    lane_lo = lax.broadcasted_iota(jnp.int32, (1, 1, LANES), 2) < RW_HEAD

    def stack_par(z3):
        return jnp.concatenate([jnp.where(lane_lo, z3, 0.0), jnp.where(lane_lo, 0.0, z3)],
                               axis=1).reshape(ROWS, LANES)

    def stack_dup(z3):
        return jnp.concatenate([z3, z3], axis=1).reshape(ROWS, LANES)

    def row_chunk(rc, carry):
        sl = pl.ds(pl.multiple_of(rc * RW_HEAD, RW_HEAD), RW_HEAD)
        lw = lw_ref[sl, :]
        c_in = _dot_exact_lhs(tril, lw, 3)
        p_in = jnp.exp(c_in)
        p_inv = jnp.exp(-c_in)
        qt = kk_ref[sl, :] * jnp.exp(c_in - lw)
        rt = r_ref[sl, :] * p_in
        kt = km_ref[sl, :] * p_inv
        bt = b_ref[sl, :] * p_inv
        vv = v_ref[sl, :]
        p_end = p_in.reshape(gb, chunk, dr)[:, chunk - 1:chunk, :]
        for j in range(pairs):
            ls = slice(j * LANES, (j + 1) * LANES)
            q3 = qt[:, ls].reshape(gb, chunk, LANES)
            r3 = rt[:, ls].reshape(gb, chunk, LANES)
            k3 = kt[:, ls].reshape(gb, chunk, LANES)
            b3 = bt[:, ls].reshape(gb, chunk, LANES)
            v3 = vv[:, ls].reshape(gb, chunk, LANES)
            lq = stack_par(q3)
            lr = stack_par(r3)
            g = _dot_nt(jnp.concatenate([lq, lr], axis=0),
                        jnp.concatenate([stack_dup(b3), stack_dup(k3)], axis=0))
            l_b = jnp.where(m_strict, g[0:ROWS, 0:ROWS], 0.0)
            m_k = jnp.where(m_strict, g[0:ROWS, ROWS:], 0.0)
            n_b = jnp.where(m_incl, g[ROWS:, 0:ROWS], 0.0)
            n_k = jnp.where(m_incl, g[ROWS:, ROWS:], 0.0)
            lq3 = lq.reshape(gb, 2 * chunk, LANES)
            lr3 = lr.reshape(gb, 2 * chunk, LANES)
            qa, ra, states = [], [], []
            for s in range(gb):
                bi = 0 if bb == 1 else (rc * RW_HEAD + s * chunk) // tt
                st = sbd_ref[bi, j]
                states.append((bi, st))
                qr = _dot_nt(jnp.concatenate([lq3[s], lr3[s]], axis=0), st)
                qa.append(qr[0:2 * chunk])
                ra.append(qr[2 * chunk:])
            qa = jnp.concatenate(qa, axis=0) if gb > 1 else qa[0]
            ra = jnp.concatenate(ra, axis=0) if gb > 1 else ra[0]
            vm = stack_par(v3)
            t_inv = eye - jnp.where(lvl_masks[0], l_b, 0.0)
            for lm in lvl_masks[1:]:
                t_inv = t_inv - _dot_hi(_dot_hi(t_inv, jnp.where(lm, l_b, 0.0)), t_inv)
            x = _dot_hi(t_inv, qa + _dot(m_k, vm))
            ys = ra + _dot(jnp.concatenate([n_k, -n_b], axis=1), jnp.concatenate([vm, x], axis=0))
            ys3 = ys.reshape(gb, 2 * chunk, LANES)
            y_ref[sl, ls] = (ys3[:, 0:chunk] + ys3[:, chunk:]).reshape(RW_HEAD, LANES)
            u3 = x.reshape(gb, 2 * chunk, LANES)
            u3 = u3[:, 0:chunk] + u3[:, chunk:]
            for s in range(gb):
                bi, st = states[s]
                ds = _dot_tn(jnp.concatenate([v3[s], -u3[s]], axis=0),
                             jnp.concatenate([k3[s], b3[s]], axis=0))
                sbd_ref[bi, j] = p_end[s][:, ls] * (st + jnp.where(m_pair, ds, 0.0))
        return carry

    lax.fori_loop(0, rows // RW_HEAD, row_chunk, 0)

    y = y_ref[...]
    inv_n = 1.0 / RW_HEAD
    d = y - head_bcast(head_sum(y) * inv_n)
    rstd = lax.rsqrt(head_sum(d * d) * inv_n + GN_EPS)
    gn = d * head_bcast(rstd) * gng_ref[...] + gnb_ref[...]
    bonus = head_bcast(head_sum(r_ref[...] * km_ref[...] * rk_ref[...])) * v_ref[...]
    o_ref[...] = ((gn + bonus) * gg).reshape(bb, tt, dr).astype(o_ref.dtype)

    @pl.when(ti == pl.num_programs(1) - 1)
    def _finish():
        def unpack(bi, carry):
            for j in range(pairs):
                ns_ref[bi, 2 * j] = sbd_ref[bi, j, 0:RW_HEAD, 0:RW_HEAD]
                ns_ref[bi, 2 * j + 1] = pltpu.roll(sbd_ref[bi, j, RW_HEAD:ROWS, :], RW_HEAD, 1)[:, 0:RW_HEAD]
            return carry

        lax.fori_loop(0, bb, unpack, 0)


def _rwkv_branch(p_rw, sbuf, s0, layer, P, *, bb, tt, chunk):
    b, t, nrw = p_rw.shape
    heads = s0.shape[1]
    dr = heads * RW_HEAD
    nl = nrw - 3 * dr
    rows = bb * tt
    assert dr % LANES == 0 and RW_HEAD % chunk == 0 and tt % chunk == 0 and rows % RW_HEAD == 0
    assert tt == chunk or bb == 1
    lw = lambda shape: pl.BlockSpec((None,) + shape, lambda i, j: (layer,) + (0,) * len(shape))
    kern = functools.partial(_rwkv_kernel, bb=bb, tt=tt, chunk=chunk, dr=dr)
    big = lambda: pltpu.VMEM((rows, dr), F32)
    return pl.pallas_call(
        kern,
        grid=(b // bb, t // tt),
        in_specs=[pl.BlockSpec((bb, tt, nrw), lambda i, j: (i, j, 0)),
                  pl.BlockSpec((bb, 1, nrw), lambda i, j: (i, 0, 0)),
                  pl.BlockSpec((bb, heads, RW_HEAD, RW_HEAD), lambda i, j: (i, 0, 0, 0)),
                  lw((1, nrw)), lw((1, dr)), lw((nl, dr)), lw((1, dr)), lw((nl, dr)), lw((nl, dr)),
                  lw((1, dr)), lw((1, dr)), lw((1, dr)), lw((1, dr)), lw((1, dr)),
                  pl.BlockSpec((dr, LANES), lambda i, j: (0, 0)),
                  pl.BlockSpec((LANES, dr), lambda i, j: (0, 0))],
        out_specs=[pl.BlockSpec((bb, tt, dr), lambda i, j: (i, j, 0)),
                   pl.BlockSpec((bb, 1, nrw), lambda i, j: (i, 0, 0)),
                   pl.BlockSpec((bb, heads, RW_HEAD, RW_HEAD), lambda i, j: (i, 0, 0, 0))],
        out_shape=[jax.ShapeDtypeStruct((b, t, dr), BF16),
                   jax.ShapeDtypeStruct((b, 1, nrw), F32),
                   jax.ShapeDtypeStruct(s0.shape, F32)],
        scratch_shapes=[pltpu.VMEM((bb, dr // LANES, ROWS, LANES), F32),
                        pltpu.VMEM((bb, 1, nrw), F32),
                        big(), big(), big(), big(), big(), big(), big(),
                        pltpu.VMEM((RW_HEAD, LANES), F32)],
        compiler_params=pltpu.CompilerParams(
            dimension_semantics=("arbitrary", "arbitrary"), vmem_limit_bytes=VMEM_LIMIT),
    )(p_rw, sbuf, s0, P["rw_mu_p"], P["rw_w0"], P["rw_w2_p"], P["rw_a0"], P["rw_a2_p"], P["rw_g2_p"],
      P["rw_kk"], P["rw_ka"], P["rw_rk"], P["rw_gn_g"], P["rw_gn_b"], P["seg"], P["segt"])


def _causal_dwconv(x, buf, w, b):
    width = w.shape[0]
    t_len = x.shape[1]
    xx = jnp.concatenate([buf.astype(x.dtype), x], axis=1)
    out = b
    for j in range(width):
        out = out + w[j] * xx[:, j:j + t_len]
    return out, xx[:, xx.shape[1] - (width - 1):]


def _trunk(x, mod, states, P, cfg):
    bsz, t_len, d = x.shape
    m = bsz * t_len
    depth = mod.shape[0]
    lru_conv, lru_h, rw_shift, rw_s, ffn_conv = states
    dl = lru_h.shape[-1]
    nrw = P["rw_mu_p"].shape[-1]
    nrw0 = rw_shift.shape[-1]
    outs = ([], [], [], [], [])
    tm, tn = cfg["tm"], cfg["tn"]
    for l in range(depth):
        mod3 = mod[l][:, None, :]
        gt1 = mod3[:, :, 2 * d:3 * d]
        gt2 = mod3[:, :, 5 * d:6 * d]

        h = _norm_mod(x, P["norm_mix"], l, mod3, 1, 0, bb=cfg["nbb"], tt=cfg["ntt"]).reshape(m, d)
        p_lru = _matmul(h, P["w_in_lru"], l, tm=tm, tn=tn, n=2 * dl).reshape(bsz, t_len, 2 * dl)
        p_rw = _matmul(h, P["w_in_rw"], l, tm=tm, tn=tn).reshape(bsz, t_len, nrw)
        p_g = _matmul(h, P["w_in_g"], l, tm=tm, tn=tn).reshape(bsz, t_len, 2 * d)

        ga, n_lru_buf, n_lru_h = _lru_branch(p_lru, lru_conv[l], lru_h[l][:, None, :], l, P,
                                             bb=cfg["lbb"], tt=cfg["ltt"])
        sbuf = jnp.pad(rw_shift[l], ((0, 0), (0, nrw - nrw0)))[:, None, :]
        o_rw, n_shift, n_s = _rwkv_branch(p_rw, sbuf, rw_s[l], l, P,
                                          bb=cfg["rbb"], tt=cfg["rtt"], chunk=cfg["chunk"])
        y_a = _matmul(ga.reshape(m, dl), P["w_pa"], l, tm=tm, tn=tn).reshape(bsz, t_len, d)
        y_b = _matmul(o_rw.reshape(m, -1), P["w_pb"], l, tm=tm, tn=tn).reshape(bsz, t_len, d)
        merged = (jax.nn.sigmoid(p_g[..., :d]) * y_a + jax.nn.sigmoid(p_g[..., d:]) * y_b).astype(BF16)
        x = x + gt1 * _matmul(merged.reshape(m, d), P["w_o"], l, tm=tm, tn=tn).reshape(bsz, t_len, d)

        h2 = _norm_mod(x, P["norm_ffn"], l, mod3, 4, 3, bb=cfg["nbb"], tt=cfg["ntt"]).reshape(m, d)
        u = _matmul(h2, P["w_up"], l, tm=tm, tn=tn).reshape(bsz, t_len, -1)
        uc, n_ffn_buf = _causal_dwconv(u, ffn_conv[l], P["ffn_conv_w"][l], P["ffn_conv_b"][l, 0])
        f = uc.shape[-1] // 2
        act = (jax.nn.silu(uc[..., :f]) * uc[..., f:]).astype(BF16)
        x = x + gt2 * _matmul(act.reshape(m, f), P["w_down"], l, tm=tm, tn=tn).reshape(bsz, t_len, d)

        for lst, ns in zip(outs, (n_lru_buf, n_lru_h[:, 0, :], n_shift[:, 0, :nrw0], n_s, n_ffn_buf)):
            lst.append(ns)
    y = _final_norm(x, P["norm_final"], bb=cfg["nbb"], tt=cfg["ntt"])
    return y, tuple(jnp.stack(lst, axis=0) for lst in outs)


def kernel(x_prompt, x_sample, c_prompt, c_sample, state_lru_conv, state_lru_h, state_rwkv_shift,
           state_rwkv_S, state_ffn_conv, w_ada, b_ada, norm_mix, norm_ffn, w_in, lru_conv_w,
           lru_conv_b, lru_wa, lru_ba, lru_wi, lru_bi, lru_lambda, w_pa, rw_mu, rw_w0, rw_w2, rw_a0,
           rw_a2, rw_g2, rw_kk, rw_ka, rw_rk, rw_gn_g, rw_gn_b, w_pb, w_o, w_up, ffn_conv_w,
           ffn_conv_b, w_down, norm_final):
    depth, d, _ = w_ada.shape
    bp, tp, _ = x_prompt.shape
    bs, ts, _ = x_sample.shape
    dl = lru_lambda.shape[-1]
    nblk, blk = lru_wa.shape[1], lru_wa.shape[2]
    heads = rw_rk.shape[1]
    dr = heads * RW_HEAD
    nrw0 = rw_mu.shape[-1]
    nl = _cdiv(nrw0 - 3 * dr, LANES) * LANES
    nrw = 3 * dr + nl
    lw_n, la_n = rw_w2.shape[1], rw_a2.shape[1]

    row = lambda p: p[:, None, :]
    eye = jnp.eye(nblk, dtype=F32)
    block_diag = lambda w: (eye[:, None, :, None] * w[:, :, :, None, :]).reshape(depth, dl, dl).astype(BF16)
    pad_rows = lambda w, off: jnp.pad(w, ((0, 0), (off, nl - off - w.shape[1]), (0, 0))).astype(BF16)
    head_of = jnp.arange(dr) // RW_HEAD
    seg = (head_of[:, None] == jnp.arange(LANES)[None, :]).astype(BF16)
    P = dict(
        norm_mix=row(norm_mix), norm_ffn=row(norm_ffn), norm_final=norm_final[None, :],
        w_in_lru=w_in,
        w_in_rw=jnp.pad(w_in[:, :, 2 * dl:2 * dl + nrw0], ((0, 0), (0, 0), (0, nrw - nrw0))),
        w_in_g=w_in[:, :, 2 * dl + nrw0:],
        lru_conv_w=lru_conv_w, lru_conv_b=row(lru_conv_b), lru_wa_bd=block_diag(lru_wa),
        lru_ba=row(lru_ba), lru_wi_bd=block_diag(lru_wi), lru_bi=row(lru_bi), lru_lambda=row(lru_lambda),
        w_pa=w_pa, w_pb=w_pb, w_o=w_o, w_up=w_up, w_down=w_down,
        rw_mu_p=row(jnp.pad(rw_mu, ((0, 0), (0, nrw - nrw0)))),
        rw_w0=row(rw_w0), rw_a0=row(rw_a0),
        rw_w2_p=pad_rows(rw_w2, 0), rw_a2_p=pad_rows(rw_a2, lw_n), rw_g2_p=pad_rows(rw_g2, lw_n + la_n),
        rw_kk=row(rw_kk), rw_ka=row(rw_ka), rw_rk=rw_rk.reshape(depth, 1, dr),
        rw_gn_g=row(rw_gn_g), rw_gn_b=row(rw_gn_b), seg=seg, segt=seg.T,
        ffn_conv_w=ffn_conv_w, ffn_conv_b=row(ffn_conv_b),
    )

    nb = bp + bs
    nb_pad = _cdiv(nb, SUBLANES) * SUBLANES
    c_all = jnp.concatenate([c_prompt, c_sample, jnp.zeros((nb_pad - nb, d), F32)], axis=0)
    mods = [_matmul(c_all, w_ada, l, tm=nb_pad, tn=1024, bias=row(b_ada), act="silu") for l in range(depth)]
    mod = jnp.stack(mods, axis=0)

    zeros = lambda *s: jnp.zeros((depth, bp) + s, F32)
    p_states = (zeros(lru_conv_w.shape[1] - 1, dl), zeros(dl), zeros(nrw0),
                zeros(heads, RW_HEAD, RW_HEAD), zeros(ffn_conv_w.shape[1] - 1, w_up.shape[-1]))
    s_states = (state_lru_conv, state_lru_h, state_rwkv_shift, state_rwkv_S, state_ffn_conv)

    cfg_p = dict(tm=512, tn=512, nbb=1, ntt=min(tp, 512), lbb=1, ltt=min(tp, 512),
                 rbb=1, rtt=min(tp, 256), chunk=min(tp, RW_HEAD))
    sb = min(bs, RW_HEAD // ts) if ts == SUBLANES else 1
    cfg_s = dict(tm=512, tn=512, nbb=min(bs, 64), ntt=ts, lbb=min(bs, 32), ltt=ts,
                 rbb=sb, rtt=ts, chunk=ts)
    y_p, st_p = _trunk(x_prompt, mod[:, :bp], p_states, P, cfg_p)
    y_s, st_s = _trunk(x_sample, mod[:, bp:nb], s_states, P, cfg_s)
    return (y_p, y_s) + st_p + st_s
```

```python
import functools
import math

import jax
import jax.numpy as jnp
from jax import lax
from jax.experimental import pallas as pl
from jax.experimental.pallas import tpu as pltpu

F32 = jnp.float32
BF16 = jnp.bfloat16

LANES = 128
SUBLANES = 8
RW_HEAD = 64
ROWS = 2 * RW_HEAD
LRU_C = 8.0
RMS_EPS = 1e-6
GN_EPS = 64e-5
VMEM_LIMIT = 56 * 1024 * 1024


def _cdiv(a, b):
    return -(-a // b)


def _dot(a, b):
    return jnp.dot(a.astype(BF16), b.astype(BF16), preferred_element_type=F32)


def _dot_nt(a, b):
    return lax.dot_general(a.astype(BF16), b.astype(BF16), (((1,), (1,)), ((), ())),
                           preferred_element_type=F32)


def _dot_tn(a, b):
    return lax.dot_general(a.astype(BF16), b.astype(BF16), (((0,), (0,)), ((), ())),
                           preferred_element_type=F32)


def _dot_hi(a, b):
    n = b.shape[1]
    a_hi, a_lo = _split(a, 2)
    b_hi, b_lo = _split(b, 2)
    lhs = jnp.concatenate([a_hi, a_lo], axis=1)
    rhs = jnp.concatenate([jnp.concatenate([b_hi, b_lo], axis=1),
                           jnp.concatenate([b_hi, jnp.zeros_like(b_lo)], axis=1)], axis=0)
    out = jnp.dot(lhs, rhs, preferred_element_type=F32)
    return out[:, :n] + out[:, n:]


def _split(x, n):
    parts = []
    for _ in range(n - 1):
        p = x.astype(BF16)
        parts.append(p)
        x = x - p.astype(F32)
    parts.append(x.astype(BF16))
    return parts


def _dot_exact_rhs(x, m, n=3):
    return sum(jnp.dot(p, m, preferred_element_type=F32) for p in _split(x, n))


def _dot_exact_lhs(m, x, n=3):
    return sum(jnp.dot(m, p, preferred_element_type=F32) for p in _split(x, n))


def _sigmoid(x):
    return 1.0 / (1.0 + jnp.exp(-x))


def _softplus(x):
    return jnp.maximum(x, 0.0) + jnp.log1p(jnp.exp(-jnp.abs(x)))


def _silu(x):
    return x * _sigmoid(x)


def _gelu_tanh(x):
    return 0.5 * x * (1.0 + jnp.tanh(math.sqrt(2.0 / math.pi) * (x + 0.044715 * (x * x * x))))


def _mm_kernel(*refs, act, bias, resid):
    refs = list(refs)
    a_ref, w_ref = refs[0], refs[1]
    o_ref, wbf_ref = refs[-2], refs[-1]
    extra = refs[2:-2]

    @pl.when(pl.program_id(1) == 0)
    def _cast_weights():
        wbf_ref[...] = w_ref[...].astype(BF16)

    a = a_ref[...]
    if act == "silu":
        a = _silu(a.astype(F32))
    acc = jnp.dot(a.astype(BF16), wbf_ref[...], preferred_element_type=F32)
    if bias:
        acc = acc + extra.pop(0)[...]
    if resid:
        x_ref, gt_ref = extra
        acc = x_ref[...] + gt_ref[...] * acc
    o_ref[...] = acc.astype(o_ref.dtype)


def _pick_tile(n, target):
    assert n % LANES == 0
    units = n // LANES
    best = max(u for u in range(1, units + 1) if units % u == 0 and u * LANES <= max(target, LANES))
    return best * LANES


def _gate_spec(gate, gate_col, t_len, tm, tn, d):
    if gate.ndim == 3:
        assert t_len % tm == 0
        nt = t_len // tm
        return pl.BlockSpec((None, 1, tn), lambda j, i: (i // nt, 0, gate_col * (d // tn) + j))
    return pl.BlockSpec((tm, tn), lambda j, i: (i, j))


def _matmul(a, w, layer, *, tm, tn, name, n=None, bias=None, act=None, resid=None, out_dtype=F32):
    m, k = a.shape
    n = w.shape[-1] if n is None else n
    tm = min(tm, m)
    tn = _pick_tile(n, tn)
    assert m % tm == 0
    grid = (n // tn, m // tm)
    in_specs = [pl.BlockSpec((tm, k), lambda j, i: (i, 0)),
                pl.BlockSpec((None, k, tn), lambda j, i: (layer, 0, j))]
    args = [a, w]
    if bias is not None:
        in_specs.append(pl.BlockSpec((None, 1, tn), lambda j, i: (layer, 0, j)))
        args.append(bias)
    if resid is not None:
        x, gate, gate_col, t_len = resid
        in_specs += [pl.BlockSpec((tm, tn), lambda j, i: (i, j)), _gate_spec(gate, gate_col, t_len, tm, tn, n)]
        args += [x, gate]
    return pl.pallas_call(
        functools.partial(_mm_kernel, act=act, bias=bias is not None, resid=resid is not None),
        grid=grid,
        in_specs=in_specs,
        out_specs=pl.BlockSpec((tm, tn), lambda j, i: (i, j)),
        out_shape=jax.ShapeDtypeStruct((m, n), out_dtype),
        scratch_shapes=[pltpu.VMEM((k, tn), BF16)],
        compiler_params=pltpu.CompilerParams(
            dimension_semantics=("arbitrary", "arbitrary"), vmem_limit_bytes=VMEM_LIMIT),
        name=name,
    )(*args)


def _ffn_up_kernel(a_ref, wg_ref, wv_ref, cwg_ref, cwv_ref, cbg_ref, cbv_ref, bg_ref, bv_ref,
                   o_ref, nbg_ref, nbv_ref, wgb_ref, wvb_ref, xg_ref, xv_ref, *, bb, tt, nt, width):
    i = pl.program_id(1)
    tn = o_ref.shape[-1]
    base = SUBLANES - (width - 1)

    @pl.when(i == 0)
    def _cast_weights():
        wgb_ref[...] = wg_ref[...].astype(BF16)
        wvb_ref[...] = wv_ref[...].astype(BF16)

    @pl.when(i % nt == 0)
    def _sequence_start():
        xg_ref[:, base:SUBLANES, :] = bg_ref[...]
        xv_ref[:, base:SUBLANES, :] = bv_ref[...]

    a = a_ref[...]

    def half(wb_ref, x_ref, cw_ref, cb_ref, nb_ref):
        x_ref[:, SUBLANES:SUBLANES + tt, :] = jnp.dot(
            a, wb_ref[...], preferred_element_type=F32).reshape(bb, tt, tn)
        conv = cb_ref[...] + cw_ref[0:1, :] * x_ref[:, base:base + tt, :]
        for j in range(1, width):
            conv = conv + cw_ref[j:j + 1, :] * x_ref[:, base + j:base + j + tt, :]
        last = x_ref[:, tt + base:tt + SUBLANES, :]
        nb_ref[...] = last
        x_ref[:, base:SUBLANES, :] = last
        return conv

    cg = half(wgb_ref, xg_ref, cwg_ref, cbg_ref, nbg_ref)
    cv = half(wvb_ref, xv_ref, cwv_ref, cbv_ref, nbv_ref)
    o_ref[...] = (_silu(cg) * cv).reshape(bb * tt, tn).astype(o_ref.dtype)


def _ffn_up(a, w, cw, cb, buf, layer, *, t_len, tm, tn, name):
    m, d = a.shape
    f = w.shape[-1] // 2
    width = cw.shape[1]
    bsz = m // t_len
    tm = min(tm, m)
    tn = _pick_tile(f, tn)
    nf = f // tn
    if tm <= t_len:
        assert t_len % tm == 0
        bb, tt = 1, tm
    else:
        assert tm % t_len == 0 and t_len == SUBLANES
        bb, tt = tm // t_len, t_len
    nt = t_len // tt
    assert t_len >= width - 1
    wspec = lambda off: pl.BlockSpec((None, d, tn), lambda j, i: (layer, 0, j + off))
    cspec = lambda rows, off: pl.BlockSpec((None, rows, tn), lambda j, i: (layer, 0, j + off))
    bspec = lambda off: pl.BlockSpec((bb, width - 1, tn), lambda j, i: (i // nt, 0, j + off))
    return pl.pallas_call(
        functools.partial(_ffn_up_kernel, bb=bb, tt=tt, nt=nt, width=width),
        grid=(nf, m // tm),
        in_specs=[pl.BlockSpec((tm, d), lambda j, i: (i, 0)), wspec(0), wspec(nf),
                  cspec(width, 0), cspec(width, nf), cspec(1, 0), cspec(1, nf), bspec(0), bspec(nf)],
        out_specs=[pl.BlockSpec((tm, tn), lambda j, i: (i, j)), bspec(0), bspec(0)],
        out_shape=[jax.ShapeDtypeStruct((m, f), BF16),
                   jax.ShapeDtypeStruct((bsz, width - 1, f), F32),
                   jax.ShapeDtypeStruct((bsz, width - 1, f), F32)],
        scratch_shapes=[pltpu.VMEM((d, tn), BF16), pltpu.VMEM((d, tn), BF16),
                        pltpu.VMEM((bb, tt + SUBLANES, tn), F32), pltpu.VMEM((bb, tt + SUBLANES, tn), F32)],
        compiler_params=pltpu.CompilerParams(
            dimension_semantics=("arbitrary", "arbitrary"), vmem_limit_bytes=VMEM_LIMIT),
        name=name,
    )(a, w, w, cw, cw, cb, cb, buf, buf)


def _norm_mod_kernel(x_ref, g_ref, sc_ref, sh_ref, o_ref):
    x = x_ref[...]
    y = x * lax.rsqrt(jnp.mean(x * x, axis=-1, keepdims=True) + RMS_EPS) * g_ref[...]
    y = y * (1.0 + sc_ref[...]) + sh_ref[...]
    o_ref[...] = y.reshape(o_ref.shape).astype(o_ref.dtype)


def _norm_mod(x, g, layer, mod3, sc_idx, sh_idx, *, bb, tt, name):
    b, t, d = x.shape
    nt = t // tt
    return pl.pallas_call(
        _norm_mod_kernel,
        grid=(b // bb, nt),
        in_specs=[pl.BlockSpec((bb, tt, d), lambda i, j: (i, j, 0)),
                  pl.BlockSpec((None, 1, d), lambda i, j: (layer, 0, 0)),
                  pl.BlockSpec((bb, 1, d), lambda i, j: (i, 0, sc_idx)),
                  pl.BlockSpec((bb, 1, d), lambda i, j: (i, 0, sh_idx))],
        out_specs=pl.BlockSpec((bb * tt, d), lambda i, j: (i * nt + j, 0)),
        out_shape=jax.ShapeDtypeStruct((b * t, d), BF16),
        compiler_params=pltpu.CompilerParams(
            dimension_semantics=("arbitrary", "arbitrary"), vmem_limit_bytes=VMEM_LIMIT),
        name=name,
    )(x, g, mod3, mod3)


def _final_norm_kernel(x_ref, g_ref, o_ref):
    x = x_ref[...]
    o_ref[...] = x * lax.rsqrt(jnp.mean(x * x, axis=-1, keepdims=True) + RMS_EPS) * g_ref[...]


def _final_norm(x, g, *, bb, tt, name):
    b, t, d = x.shape
    return pl.pallas_call(
        _final_norm_kernel,
        grid=(b // bb, t // tt),
        in_specs=[pl.BlockSpec((bb, tt, d), lambda i, j: (i, j, 0)),
                  pl.BlockSpec((1, d), lambda i, j: (0, 0))],
        out_specs=pl.BlockSpec((bb, tt, d), lambda i, j: (i, j, 0)),
        out_shape=jax.ShapeDtypeStruct((b, t, d), F32),
        compiler_params=pltpu.CompilerParams(
            dimension_semantics=("arbitrary", "arbitrary"), vmem_limit_bytes=VMEM_LIMIT),
        name=name,
    )(x, g)


def _lru_kernel(x_ref, gate_ref, buf_ref, h0_ref, cw_ref, cb_ref, wa_ref, ba_ref, wi_ref, bi_ref,
                lam_ref, ga_ref, nbuf_ref, nh_ref, xx_ref, hc_ref, a_ref, b_ref, h_ref, *, bb, tt, width):
    c = x_ref.shape[-1]
    rows = bb * tt
    halo = width - 1
    base = SUBLANES - halo

    @pl.when(pl.program_id(1) == 0)
    def _init():
        xx_ref[:, base:SUBLANES, :] = buf_ref[...]
        hc_ref[...] = h0_ref[...]

    xx_ref[:, SUBLANES:SUBLANES + tt, :] = x_ref[...]
    conv = cb_ref[...] + cw_ref[0:1, :] * xx_ref[:, base:base + tt, :]
    for j in range(1, width):
        conv = conv + cw_ref[j:j + 1, :] * xx_ref[:, base + j:base + j + tt, :]
    last = xx_ref[:, tt + base:tt + SUBLANES, :]
    nbuf_ref[...] = last
    xx_ref[:, base:SUBLANES, :] = last

    xc = conv.reshape(rows, c)
    r = _sigmoid(_dot(xc, wa_ref[...]) + ba_ref[...])
    i = _sigmoid(_dot(xc, wi_ref[...]) + bi_ref[...])
    log_a = -LRU_C * r * _softplus(-lam_ref[...])
    a = jnp.exp(log_a)
    inp = jnp.sqrt(-jnp.tanh(log_a) * (a * a + 1.0)) * (i * xc)

    pos = lax.broadcasted_iota(jnp.int32, (rows, 1), 0) % SUBLANES
    for s in (1, 2, 4):
        a_sh = pltpu.roll(a, s, 0)
        b_sh = pltpu.roll(inp, s, 0)
        m = pos >= s
        inp = jnp.where(m, a * b_sh + inp, inp)
        a = jnp.where(m, a * a_sh, a)

    if tt == SUBLANES:
        h = (inp.reshape(bb, tt, c) + a.reshape(bb, tt, c) * hc_ref[...]).reshape(rows, c)
    else:
        a_ref[...] = a
        b_ref[...] = inp

        def group(g, carry):
            sl = pl.ds(pl.multiple_of(g * SUBLANES, SUBLANES), SUBLANES)
            hg = b_ref[sl, :] + a_ref[sl, :] * carry
            h_ref[sl, :] = hg
            return hg[SUBLANES - 1:SUBLANES, :]

        lax.fori_loop(0, rows // SUBLANES, group, hc_ref[0])
        h = h_ref[...]

    h3 = h.reshape(bb, tt, c)
    hc_ref[...] = h3[:, tt - 1:tt, :]
    nh_ref[...] = h3[:, tt - 1:tt, :]
    ga_ref[...] = (_gelu_tanh(gate_ref[...]) * h3).reshape(rows, c).astype(ga_ref.dtype)


def _lru_branch(p_lru, buf, h0, layer, P, *, bb, tt, name):
    b, t, c2 = p_lru.shape
    c = c2 // 2
    width = P["lru_conv_w"].shape[1]
    assert t >= width - 1 and (tt == SUBLANES or bb == 1)
    lw = lambda shape: pl.BlockSpec((None,) + shape, lambda i, j: (layer,) + (0,) * len(shape))
    kern = functools.partial(_lru_kernel, bb=bb, tt=tt, width=width)
    return pl.pallas_call(
        kern,
        grid=(b // bb, t // tt),
        in_specs=[pl.BlockSpec((bb, tt, c), lambda i, j: (i, j, 0)),
                  pl.BlockSpec((bb, tt, c), lambda i, j: (i, j, 1)),
                  pl.BlockSpec((bb, width - 1, c), lambda i, j: (i, 0, 0)),
                  pl.BlockSpec((bb, 1, c), lambda i, j: (i, 0, 0)),
                  lw((width, c)), lw((1, c)), lw((c, c)), lw((1, c)), lw((c, c)), lw((1, c)), lw((1, c))],
        out_specs=[pl.BlockSpec((bb * tt, c), lambda i, j: (i * (t // tt) + j, 0)),
                   pl.BlockSpec((bb, width - 1, c), lambda i, j: (i, 0, 0)),
                   pl.BlockSpec((bb, 1, c), lambda i, j: (i, 0, 0))],
        out_shape=[jax.ShapeDtypeStruct((b * t, c), BF16),
                   jax.ShapeDtypeStruct((b, width - 1, c), F32),
                   jax.ShapeDtypeStruct((b, 1, c), F32)],
        scratch_shapes=[pltpu.VMEM((bb, tt + SUBLANES, c), F32),
                        pltpu.VMEM((bb, 1, c), F32),
                        pltpu.VMEM((bb * tt, c), F32),
                        pltpu.VMEM((bb * tt, c), F32),
                        pltpu.VMEM((bb * tt, c), F32)],
        compiler_params=pltpu.CompilerParams(
            dimension_semantics=("arbitrary", "arbitrary"), vmem_limit_bytes=VMEM_LIMIT),
        name=name,
    )(p_lru, p_lru, buf, h0, P["lru_conv_w"], P["lru_conv_b"], P["lru_wa_bd"], P["lru_ba"],
      P["lru_wi_bd"], P["lru_bi"], P["lru_lambda"])


def _rwkv_kernel(x_ref, sbuf_ref, s0_ref, mu_ref, w0_ref, w2_ref, a0_ref, a2_ref, g2_ref, kkp_ref,
                 ka_ref, rk_ref, gng_ref, gnb_ref, seg_ref, segt_ref,
                 o_ref, nshift_ref, ns_ref,
                 sbd_ref, prev_ref, r_ref, kk_ref, km_ref, b_ref, v_ref, lw_ref, y_ref, tmp_ref,
                 *, bb, tt, chunk, dr):
    rows = bb * tt
    nrw = x_ref.shape[-1]
    pairs = dr // LANES
    gb = RW_HEAD // chunk
    ti = pl.program_id(1)

    @pl.when(ti == 0)
    def _init():
        prev_ref[...] = sbuf_ref[...]
        tmp_ref[...] = jnp.zeros_like(tmp_ref)

        def pack(bi, carry):
            for j in range(pairs):
                sbd_ref[bi, j, 0:RW_HEAD, 0:RW_HEAD] = s0_ref[bi, 2 * j]
                sbd_ref[bi, j, 0:RW_HEAD, RW_HEAD:LANES] = jnp.zeros((RW_HEAD, RW_HEAD), F32)
                tmp_ref[:, 0:RW_HEAD] = s0_ref[bi, 2 * j + 1]
                sbd_ref[bi, j, RW_HEAD:ROWS, :] = pltpu.roll(tmp_ref[...], RW_HEAD, 1)
            return carry

        lax.fori_loop(0, bb, pack, 0)

    x3 = x_ref[...]
    rolled = pltpu.roll(x3.reshape(rows, nrw), 1, 0).reshape(bb, tt, nrw)
    t_pos = lax.broadcasted_iota(jnp.int32, (bb, tt, 1), 1)
    prev3 = jnp.where(t_pos == 0, prev_ref[...], rolled)
    last = x3[:, tt - 1:tt, :]
    prev_ref[...] = last
    nshift_ref[...] = last
    xs = (x3 + (prev3 - x3) * mu_ref[...]).reshape(rows, nrw)

    r = xs[:, 0:dr]
    k = xs[:, dr:2 * dr]
    v = xs[:, 2 * dr:3 * dr]
    lora = xs[:, 3 * dr:]
    w_log = -_softplus(-(w0_ref[...] + _dot(jnp.tanh(lora), w2_ref[...]))) - 0.5
    a = _sigmoid(a0_ref[...] + _dot(lora, a2_ref[...]))
    gg = _dot(_sigmoid(lora), g2_ref[...])

    seg = seg_ref[...]
    segt = segt_ref[...]
    head_sum = lambda z: _dot_exact_rhs(z, seg, 3)
    head_bcast = lambda z: _dot_exact_rhs(z, segt, 2)

    kk = k * kkp_ref[...]
    kk = kk * head_bcast(1.0 / jnp.maximum(jnp.sqrt(head_sum(kk * kk)), 1e-12))
    km = k * (1.0 + (a - 1.0) * ka_ref[...])
    r_ref[...] = r
    kk_ref[...] = kk
    km_ref[...] = km
    b_ref[...] = kk * a
    v_ref[...] = v
    lw_ref[...] = -jnp.exp(w_log)

    ri = lax.broadcasted_iota(jnp.int32, (ROWS, ROWS), 0)
    ci = lax.broadcasted_iota(jnp.int32, (ROWS, ROWS), 1)
    same = (ri // chunk) == (ci // chunk)
    m_strict = same & ((ci % chunk) < (ri % chunk))
    m_incl = same & ((ci % chunk) <= (ri % chunk))
    m_pair = (ri // RW_HEAD) == (ci // RW_HEAD)
    eye = jnp.where(ri == ci, 1.0, 0.0)
    lvl_masks = []
    s = 1
    while s < chunk:
        lvl_masks.append(((ri // (2 * s)) == (ci // (2 * s))) & ((ri // s) != (ci // s)))
        s *= 2
    r64 = lax.broadcasted_iota(jnp.int32, (RW_HEAD, RW_HEAD), 0)
    c64 = lax.broadcasted_iota(jnp.int32, (RW_HEAD, RW_HEAD), 1)
    tril = jnp.where(((r64 // chunk) == (c64 // chunk)) & (c64 <= r64), 1.0, 0.0).astype(BF16)
    lane_lo = lax.broadcasted_iota(jnp.int32, (1, 1, LANES), 2) < RW_HEAD

    def stack_par(z3):
        return jnp.concatenate([jnp.where(lane_lo, z3, 0.0), jnp.where(lane_lo, 0.0, z3)],
                               axis=1).reshape(ROWS, LANES)

    def stack_dup(z3):
        return jnp.concatenate([z3, z3], axis=1).reshape(ROWS, LANES)

    def row_chunk(rc, carry):
        sl = pl.ds(pl.multiple_of(rc * RW_HEAD, RW_HEAD), RW_HEAD)
        lw = lw_ref[sl, :]
        c_in = _dot_exact_lhs(tril, lw, 3)
        p_in = jnp.exp(c_in)
        p_inv = jnp.exp(-c_in)
        qt = kk_ref[sl, :] * jnp.exp(c_in - lw)
        rt = r_ref[sl, :] * p_in
        kt = km_ref[sl, :] * p_inv
        bt = b_ref[sl, :] * p_inv
        vv = v_ref[sl, :]
        p_end = p_in.reshape(gb, chunk, dr)[:, chunk - 1:chunk, :]
        pr = range(pairs)
        lanes = [slice(j * LANES, (j + 1) * LANES) for j in pr]
        split3 = lambda z: [z[:, ls].reshape(gb, chunk, LANES) for ls in lanes]
        q3, r3, k3, b3, v3 = split3(qt), split3(rt), split3(kt), split3(bt), split3(vv)
        lq = [stack_par(z) for z in q3]
        lr = [stack_par(z) for z in r3]
        vm = [stack_par(z) for z in v3]
        g = [_dot_nt(jnp.concatenate([lq[j], lr[j]], axis=0),
                     jnp.concatenate([stack_dup(b3[j]), stack_dup(k3[j])], axis=0)) for j in pr]
        l_b = [jnp.where(m_strict, g[j][0:ROWS, 0:ROWS], 0.0) for j in pr]
        m_k = [jnp.where(m_strict, g[j][0:ROWS, ROWS:], 0.0) for j in pr]
        n_kb = [jnp.concatenate([jnp.where(m_incl, g[j][ROWS:, ROWS:], 0.0),
                                 jnp.where(m_incl, -g[j][ROWS:, 0:ROWS], 0.0)], axis=1) for j in pr]
        mkv = [_dot(m_k[j], vm[j]) for j in pr]
        t_inv = [eye - jnp.where(lvl_masks[0], l_b[j], 0.0) for j in pr]
        for lm in lvl_masks[1:]:
            w = [_dot_hi(t_inv[j], jnp.where(lm, l_b[j], 0.0)) for j in pr]
            t_inv = [t_inv[j] - _dot_hi(w[j], t_inv[j]) for j in pr]
        bis = [0 if bb == 1 else (rc * RW_HEAD + s * chunk) // tt for s in range(gb)]
        st = [[sbd_ref[bis[s], j] for s in range(gb)] for j in pr]
        qa, ra = [], []
        for j in pr:
            lq3 = lq[j].reshape(gb, 2 * chunk, LANES)
            lr3 = lr[j].reshape(gb, 2 * chunk, LANES)
            qr = [_dot_nt(jnp.concatenate([lq3[s], lr3[s]], axis=0), st[j][s]) for s in range(gb)]
            qa.append(jnp.concatenate([z[0:2 * chunk] for z in qr], axis=0) if gb > 1 else qr[0][0:2 * chunk])
            ra.append(jnp.concatenate([z[2 * chunk:] for z in qr], axis=0) if gb > 1 else qr[0][2 * chunk:])
        x = [_dot_hi(t_inv[j], qa[j] + mkv[j]) for j in pr]
        ys = [ra[j] + _dot(n_kb[j], jnp.concatenate([vm[j], x[j]], axis=0)) for j in pr]
        for j in pr:
            ys3 = ys[j].reshape(gb, 2 * chunk, LANES)
            y_ref[sl, lanes[j]] = (ys3[:, 0:chunk] + ys3[:, chunk:]).reshape(RW_HEAD, LANES)
        for j in pr:
            u3 = x[j].reshape(gb, 2 * chunk, LANES)
            u3 = u3[:, 0:chunk] + u3[:, chunk:]
            for s in range(gb):
                ds = _dot_tn(jnp.concatenate([v3[j][s], -u3[s]], axis=0),
                             jnp.concatenate([k3[j][s], b3[j][s]], axis=0))
                sbd_ref[bis[s], j] = p_end[s][:, lanes[j]] * (st[j][s] + jnp.where(m_pair, ds, 0.0))
        return carry

    lax.fori_loop(0, rows // RW_HEAD, row_chunk, 0)

    y = y_ref[...]
    inv_n = 1.0 / RW_HEAD
    d = y - head_bcast(head_sum(y) * inv_n)
    rstd = lax.rsqrt(head_sum(d * d) * inv_n + GN_EPS)
    gn = d * head_bcast(rstd) * gng_ref[...] + gnb_ref[...]
    bonus = head_bcast(head_sum(r_ref[...] * km_ref[...] * rk_ref[...])) * v_ref[...]
    o_ref[...] = ((gn + bonus) * gg).astype(o_ref.dtype)

    @pl.when(ti == pl.num_programs(1) - 1)
    def _finish():
        def unpack(bi, carry):
            for j in range(pairs):
                ns_ref[bi, 2 * j] = sbd_ref[bi, j, 0:RW_HEAD, 0:RW_HEAD]
                ns_ref[bi, 2 * j + 1] = pltpu.roll(sbd_ref[bi, j, RW_HEAD:ROWS, :], RW_HEAD, 1)[:, 0:RW_HEAD]
            return carry

        lax.fori_loop(0, bb, unpack, 0)


def _rwkv_branch(p_rw, sbuf, s0, layer, P, *, bb, tt, chunk, name):
    b, t, nrw = p_rw.shape
    heads = s0.shape[1]
    dr = heads * RW_HEAD
    nl = nrw - 3 * dr
    rows = bb * tt
    assert dr % LANES == 0 and RW_HEAD % chunk == 0 and tt % chunk == 0 and rows % RW_HEAD == 0
    assert tt == chunk or bb == 1
    lw = lambda shape: pl.BlockSpec((None,) + shape, lambda i, j: (layer,) + (0,) * len(shape))
    kern = functools.partial(_rwkv_kernel, bb=bb, tt=tt, chunk=chunk, dr=dr)
    big = lambda: pltpu.VMEM((rows, dr), F32)
    return pl.pallas_call(
        kern,
        grid=(b // bb, t // tt),
        in_specs=[pl.BlockSpec((bb, tt, nrw), lambda i, j: (i, j, 0)),
                  pl.BlockSpec((bb, 1, nrw), lambda i, j: (i, 0, 0)),
                  pl.BlockSpec((bb, heads, RW_HEAD, RW_HEAD), lambda i, j: (i, 0, 0, 0)),
                  lw((1, nrw)), lw((1, dr)), lw((nl, dr)), lw((1, dr)), lw((nl, dr)), lw((nl, dr)),
                  lw((1, dr)), lw((1, dr)), lw((1, dr)), lw((1, dr)), lw((1, dr)),
                  pl.BlockSpec((dr, LANES), lambda i, j: (0, 0)),
                  pl.BlockSpec((LANES, dr), lambda i, j: (0, 0))],
        out_specs=[pl.BlockSpec((rows, dr), lambda i, j: (i * (t // tt) + j, 0)),
                   pl.BlockSpec((bb, 1, nrw), lambda i, j: (i, 0, 0)),
                   pl.BlockSpec((bb, heads, RW_HEAD, RW_HEAD), lambda i, j: (i, 0, 0, 0))],
        out_shape=[jax.ShapeDtypeStruct((b * t, dr), BF16),
                   jax.ShapeDtypeStruct((b, 1, nrw), F32),
                   jax.ShapeDtypeStruct(s0.shape, F32)],
        scratch_shapes=[pltpu.VMEM((bb, dr // LANES, ROWS, LANES), F32),
                        pltpu.VMEM((bb, 1, nrw), F32),
                        big(), big(), big(), big(), big(), big(), big(),
                        pltpu.VMEM((RW_HEAD, LANES), F32)],
        compiler_params=pltpu.CompilerParams(
            dimension_semantics=("arbitrary", "arbitrary"), vmem_limit_bytes=VMEM_LIMIT),
        name=name,
    )(p_rw, sbuf, s0, P["rw_mu_p"], P["rw_w0"], P["rw_w2_p"], P["rw_a0"], P["rw_a2_p"], P["rw_g2_p"],
      P["rw_kk"], P["rw_ka"], P["rw_rk"], P["rw_gn_g"], P["rw_gn_b"], P["seg"], P["segt"])


def _trunk(x, mod, states, P, cfg, tag):
    bsz, t_len, d = x.shape
    m = bsz * t_len
    depth = mod.shape[0]
    lru_conv, lru_h, rw_shift, rw_s, ffn_conv = states
    dl = lru_h.shape[-1]
    nrw = P["rw_mu_p"].shape[-1]
    nrw0 = rw_shift.shape[-1]
    outs = ([], [], [], [], [])
    tm, tn = cfg["tm"], cfg["tn"]
    mm = functools.partial(_matmul, tm=tm, tn=tn)
    x = x.reshape(m, d)
    for l in range(depth):
        mod3 = mod[l][:, None, :]
        if t_len % min(tm, m) == 0:
            gate1, gate2 = (mod3, 2), (mod3, 5)
        else:
            gate1 = (jnp.repeat(mod[l][:, 2 * d:3 * d], t_len, axis=0), 0)
            gate2 = (jnp.repeat(mod[l][:, 5 * d:6 * d], t_len, axis=0), 0)
        norm = functools.partial(_norm_mod, bb=cfg["nbb"], tt=cfg["ntt"])

        h = norm(x.reshape(bsz, t_len, d), P["norm_mix"], l, mod3, 1, 0, name=f"norm_mix_{tag}")
        p_lru = mm(h, P["w_in_lru"], l, n=2 * dl, name=f"in_lru_{tag}").reshape(bsz, t_len, 2 * dl)
        p_rw = _matmul(h, P["w_in_rw"], l, tm=tm, tn=cfg["tn_rw"], name=f"in_rw_{tag}").reshape(bsz, t_len, nrw)
        p_g = mm(h, P["w_in_g"], l, name=f"in_gates_{tag}")

        ga, n_lru_buf, n_lru_h = _lru_branch(p_lru, lru_conv[l], lru_h[l][:, None, :], l, P,
                                             bb=cfg["lbb"], tt=cfg["ltt"], name=f"lru_{tag}")
        sbuf = jnp.pad(rw_shift[l], ((0, 0), (0, nrw - nrw0)))[:, None, :]
        o_rw, n_shift, n_s = _rwkv_branch(p_rw, sbuf, rw_s[l], l, P, bb=cfg["rbb"], tt=cfg["rtt"],
                                          chunk=cfg["chunk"], name=f"rwkv_{tag}")
        y_a = mm(ga, P["w_pa"], l, name=f"pa_{tag}")
        y_b = mm(o_rw, P["w_pb"], l, name=f"pb_{tag}")
        merged = (jax.nn.sigmoid(p_g[:, :d]) * y_a + jax.nn.sigmoid(p_g[:, d:]) * y_b).astype(BF16)
        x = _matmul(merged, P["w_o"], l, tm=tm, tn=cfg["tn_res"], resid=(x,) + gate1 + (t_len,), name=f"o_{tag}")

        h2 = norm(x.reshape(bsz, t_len, d), P["norm_ffn"], l, mod3, 4, 3, name=f"norm_ffn_{tag}")
        act, n_buf_g, n_buf_v = _ffn_up(h2, P["w_up"], P["ffn_conv_w"], P["ffn_conv_b"], ffn_conv[l], l,
                                        t_len=t_len, tm=tm, tn=cfg["tn_up"], name=f"ffn_up_{tag}")
        x = _matmul(act, P["w_down"], l, tm=cfg["tm_down"], tn=cfg["tn_res"], resid=(x,) + gate2 + (t_len,),
                    name=f"ffn_down_{tag}")

        n_ffn_buf = jnp.concatenate([n_buf_g, n_buf_v], axis=-1)
        for lst, ns in zip(outs, (n_lru_buf, n_lru_h[:, 0, :], n_shift[:, 0, :nrw0], n_s, n_ffn_buf)):
            lst.append(ns)
    y = _final_norm(x.reshape(bsz, t_len, d), P["norm_final"], bb=cfg["nbb"], tt=cfg["ntt"], name=f"norm_final_{tag}")
    return y, tuple(jnp.stack(lst, axis=0) for lst in outs)


def kernel(x_prompt, x_sample, c_prompt, c_sample, state_lru_conv, state_lru_h, state_rwkv_shift,
           state_rwkv_S, state_ffn_conv, w_ada, b_ada, norm_mix, norm_ffn, w_in, lru_conv_w,
           lru_conv_b, lru_wa, lru_ba, lru_wi, lru_bi, lru_lambda, w_pa, rw_mu, rw_w0, rw_w2, rw_a0,
           rw_a2, rw_g2, rw_kk, rw_ka, rw_rk, rw_gn_g, rw_gn_b, w_pb, w_o, w_up, ffn_conv_w,
           ffn_conv_b, w_down, norm_final):
    depth, d, _ = w_ada.shape
    bp, tp, _ = x_prompt.shape
    bs, ts, _ = x_sample.shape
    dl = lru_lambda.shape[-1]
    nblk, blk = lru_wa.shape[1], lru_wa.shape[2]
    heads = rw_rk.shape[1]
    dr = heads * RW_HEAD
    nrw0 = rw_mu.shape[-1]
    nl = _cdiv(nrw0 - 3 * dr, LANES) * LANES
    nrw = 3 * dr + nl
    lw_n, la_n = rw_w2.shape[1], rw_a2.shape[1]

    row = lambda p: p[:, None, :]
    eye = jnp.eye(nblk, dtype=F32)
    block_diag = lambda w: (eye[:, None, :, None] * w[:, :, :, None, :]).reshape(depth, dl, dl).astype(BF16)
    pad_rows = lambda w, off: jnp.pad(w, ((0, 0), (off, nl - off - w.shape[1]), (0, 0))).astype(BF16)
    head_of = jnp.arange(dr) // RW_HEAD
    seg = (head_of[:, None] == jnp.arange(LANES)[None, :]).astype(BF16)
    P = dict(
        norm_mix=row(norm_mix), norm_ffn=row(norm_ffn), norm_final=norm_final[None, :],
        w_in_lru=w_in,
        w_in_rw=jnp.pad(w_in[:, :, 2 * dl:2 * dl + nrw0], ((0, 0), (0, 0), (0, nrw - nrw0))),
        w_in_g=w_in[:, :, 2 * dl + nrw0:],
        lru_conv_w=lru_conv_w, lru_conv_b=row(lru_conv_b), lru_wa_bd=block_diag(lru_wa),
        lru_ba=row(lru_ba), lru_wi_bd=block_diag(lru_wi), lru_bi=row(lru_bi), lru_lambda=row(lru_lambda),
        w_pa=w_pa, w_pb=w_pb, w_o=w_o, w_up=w_up, w_down=w_down,
        rw_mu_p=row(jnp.pad(rw_mu, ((0, 0), (0, nrw - nrw0)))),
        rw_w0=row(rw_w0), rw_a0=row(rw_a0),
        rw_w2_p=pad_rows(rw_w2, 0), rw_a2_p=pad_rows(rw_a2, lw_n), rw_g2_p=pad_rows(rw_g2, lw_n + la_n),
        rw_kk=row(rw_kk), rw_ka=row(rw_ka), rw_rk=rw_rk.reshape(depth, 1, dr),
        rw_gn_g=row(rw_gn_g), rw_gn_b=row(rw_gn_b), seg=seg, segt=seg.T,
        ffn_conv_w=ffn_conv_w, ffn_conv_b=row(ffn_conv_b),
    )

    nb = bp + bs
    nb_pad = _cdiv(nb, SUBLANES) * SUBLANES
    c_all = jnp.concatenate([c_prompt, c_sample, jnp.zeros((nb_pad - nb, d), F32)], axis=0)
    mods = [_matmul(c_all, w_ada, l, tm=nb_pad, tn=1024, bias=row(b_ada), act="silu", name="adaln")
            for l in range(depth)]
    mod = jnp.stack(mods, axis=0)

    zeros = lambda *s: jnp.zeros((depth, bp) + s, F32)
    p_states = (zeros(lru_conv_w.shape[1] - 1, dl), zeros(dl), zeros(nrw0),
                zeros(heads, RW_HEAD, RW_HEAD), zeros(ffn_conv_w.shape[1] - 1, w_up.shape[-1]))
    s_states = (state_lru_conv, state_lru_h, state_rwkv_shift, state_rwkv_S, state_ffn_conv)

    tiles = dict(tm=1024, tn=1024, tn_rw=1152, tn_res=512, tn_up=512, tm_down=512)
    cfg_p = dict(tiles, nbb=1, ntt=min(tp, 512), lbb=1, ltt=min(tp, 512),
                 rbb=1, rtt=min(tp, 256), chunk=min(tp, RW_HEAD))
    sb = min(bs, RW_HEAD // ts) if ts == SUBLANES else 1
    cfg_s = dict(tiles, nbb=min(bs, 64), ntt=ts, lbb=min(bs, 32), ltt=ts, rbb=sb, rtt=ts, chunk=ts)
    y_p, st_p = _trunk(x_prompt, mod[:, :bp], p_states, P, cfg_p, "prompt")
    y_s, st_s = _trunk(x_sample, mod[:, bp:nb], s_states, P, cfg_s, "sample")
    return (y_p, y_s) + st_p + st_s
```

```python
import functools
import math

import jax
import jax.numpy as jnp
from jax import lax
from jax.experimental import pallas as pl
from jax.experimental.pallas import tpu as pltpu

F32 = jnp.float32
BF16 = jnp.bfloat16

LANES = 128
SUBLANES = 8
RW_HEAD = 64
ROWS = 2 * RW_HEAD
LRU_C = 8.0
RMS_EPS = 1e-6
GN_EPS = 64e-5
VMEM_LIMIT = 56 * 1024 * 1024


def _cdiv(a, b):
    return -(-a // b)


def _dot(a, b):
    return jnp.dot(a.astype(BF16), b.astype(BF16), preferred_element_type=F32)


def _dot_nt(a, b):
    return lax.dot_general(a.astype(BF16), b.astype(BF16), (((1,), (1,)), ((), ())),
                           preferred_element_type=F32)


def _dot_tn(a, b):
    return lax.dot_general(a.astype(BF16), b.astype(BF16), (((0,), (0,)), ((), ())),
                           preferred_element_type=F32)


def _dot_hi(a, b):
    n = b.shape[1]
    a_hi, a_lo = _split(a, 2)
    b_hi, b_lo = _split(b, 2)
    lhs = jnp.concatenate([a_hi, a_lo], axis=1)
    rhs = jnp.concatenate([jnp.concatenate([b_hi, b_lo], axis=1),
                           jnp.concatenate([b_hi, jnp.zeros_like(b_lo)], axis=1)], axis=0)
    out = jnp.dot(lhs, rhs, preferred_element_type=F32)
    return out[:, :n] + out[:, n:]


def _split(x, n):
    parts = []
    for _ in range(n - 1):
        p = x.astype(BF16)
        parts.append(p)
        x = x - p.astype(F32)
    parts.append(x.astype(BF16))
    return parts


def _dot_exact_rhs(x, m, n=3):
    return sum(jnp.dot(p, m, preferred_element_type=F32) for p in _split(x, n))


def _dot_exact_lhs(m, x, n=3):
    return sum(jnp.dot(m, p, preferred_element_type=F32) for p in _split(x, n))


def _sigmoid(x):
    return 0.5 * jnp.tanh(0.5 * x) + 0.5


def _softplus(x):
    return jnp.maximum(x, 0.0) + jnp.log1p(jnp.exp(-jnp.abs(x)))


def _silu(x):
    return x * _sigmoid(x)


def _gelu_tanh(x):
    return 0.5 * x * (1.0 + jnp.tanh(math.sqrt(2.0 / math.pi) * (x + 0.044715 * (x * x * x))))


def _mm_kernel(*refs, act, bias, resid):
    refs = list(refs)
    a_ref, w_ref = refs[0], refs[1]
    o_ref, wbf_ref = refs[-2], refs[-1]
    extra = refs[2:-2]

    @pl.when(pl.program_id(1) == 0)
    def _cast_weights():
        wbf_ref[...] = w_ref[...].astype(BF16)

    a = a_ref[...]
    if act == "silu":
        a = _silu(a.astype(F32))
    acc = jnp.dot(a.astype(BF16), wbf_ref[...], preferred_element_type=F32)
    if bias:
        acc = acc + extra.pop(0)[...]
    if resid:
        x_ref, gt_ref = extra
        acc = x_ref[...] + gt_ref[...] * acc
    o_ref[...] = acc.astype(o_ref.dtype)


def _pick_tile(n, target):
    assert n % LANES == 0
    units = n // LANES
    best = max(u for u in range(1, units + 1) if units % u == 0 and u * LANES <= max(target, LANES))
    return best * LANES


def _gate_spec(gate, gate_col, t_len, tm, tn, d):
    if gate.ndim == 3:
        assert t_len % tm == 0
        nt = t_len // tm
        return pl.BlockSpec((None, 1, tn), lambda j, i: (i // nt, 0, gate_col * (d // tn) + j))
    return pl.BlockSpec((tm, tn), lambda j, i: (i, j))


def _matmul(a, w, layer, *, tm, tn, name, n=None, bias=None, act=None, resid=None, out_dtype=F32):
    m, k = a.shape
    n = w.shape[-1] if n is None else n
    tm = min(tm, m)
    tn = _pick_tile(n, tn)
    assert m % tm == 0
    grid = (n // tn, m // tm)
    in_specs = [pl.BlockSpec((tm, k), lambda j, i: (i, 0)),
                pl.BlockSpec((None, k, tn), lambda j, i: (layer, 0, j))]
    args = [a, w]
    if bias is not None:
        in_specs.append(pl.BlockSpec((None, 1, tn), lambda j, i: (layer, 0, j)))
        args.append(bias)
    if resid is not None:
        x, gate, gate_col, t_len = resid
        in_specs += [pl.BlockSpec((tm, tn), lambda j, i: (i, j)), _gate_spec(gate, gate_col, t_len, tm, tn, n)]
        args += [x, gate]
    return pl.pallas_call(
        functools.partial(_mm_kernel, act=act, bias=bias is not None, resid=resid is not None),
        grid=grid,
        in_specs=in_specs,
        out_specs=pl.BlockSpec((tm, tn), lambda j, i: (i, j)),
        out_shape=jax.ShapeDtypeStruct((m, n), out_dtype),
        scratch_shapes=[pltpu.VMEM((k, tn), BF16)],
        compiler_params=pltpu.CompilerParams(
            dimension_semantics=("arbitrary", "arbitrary"), vmem_limit_bytes=VMEM_LIMIT),
        name=name,
    )(*args)


def _in_proj_kernel(a_ref, wa_ref, wb_ref, o_ref, wbf_ref, *, n_plain, shift):
    j = pl.program_id(0)
    first_row_tile = pl.program_id(1) == 0
    tn = wa_ref.shape[1]

    @pl.when(first_row_tile & (j < n_plain))
    def _cast_weights():
        wbf_ref[...] = wa_ref[...].astype(BF16)

    @pl.when(first_row_tile & (j >= n_plain))
    def _cast_shifted_weights():
        if shift == 0:
            wbf_ref[...] = wa_ref[...].astype(BF16)
            return
        k = wa_ref.shape[0]
        rc = math.gcd(k, 256)
        reps = tn // wb_ref.shape[1]
        lane = lax.broadcasted_iota(jnp.int32, (1, tn), 1)

        def rows(c, carry):
            sl = pl.ds(pl.multiple_of(c * rc, rc), rc)
            wb = wb_ref[sl, :]
            wb = jnp.concatenate([wb] * reps, axis=1) if reps > 1 else wb
            w = jnp.where(lane < tn - shift, pltpu.roll(wa_ref[sl, :], tn - shift, 1),
                          pltpu.roll(wb, tn - shift, 1))
            wbf_ref[sl, :] = w.astype(BF16)
            return carry

        lax.fori_loop(0, k // rc, rows, 0)

    o_ref[...] = jnp.dot(a_ref[...], wbf_ref[...], preferred_element_type=F32)


def _in_proj(a, w_in, layer, *, n_lru, n_rw, n_gates, tm, tn, name):
    m, k = a.shape
    n_in = w_in.shape[-1]
    tm = min(tm, m)
    tn = _pick_tile(math.gcd(n_lru, n_gates), tn)
    n_rw_t = _cdiv(n_rw, tn) * tn
    gates_start = n_in - n_gates
    assert m % tm == 0 and n_lru + n_rw_t <= n_in
    n_plain = (n_lru + n_rw_t) // tn
    base, shift = gates_start // tn, gates_start % tn
    wbw = tn // 2 if (tn // 2) % LANES == 0 and shift <= tn // 2 else tn
    wa_idx = lambda j: jnp.where(j < n_plain, j, j - n_plain + base)
    wb_idx = lambda j: (jnp.maximum(j - n_plain, 0) + base + (1 if shift else 0)) * (tn // wbw)
    n_out = n_lru + n_rw_t + n_gates
    return pl.pallas_call(
        functools.partial(_in_proj_kernel, n_plain=n_plain, shift=shift),
        grid=(n_out // tn, m // tm),
        in_specs=[pl.BlockSpec((tm, k), lambda j, i: (i, 0)),
                  pl.BlockSpec((None, k, tn), lambda j, i: (layer, 0, wa_idx(j))),
                  pl.BlockSpec((None, k, wbw), lambda j, i: (layer, 0, wb_idx(j)))],
        out_specs=pl.BlockSpec((tm, tn), lambda j, i: (i, j)),
        out_shape=jax.ShapeDtypeStruct((m, n_out), F32),
        scratch_shapes=[pltpu.VMEM((k, tn), BF16)],
        compiler_params=pltpu.CompilerParams(
            dimension_semantics=("arbitrary", "arbitrary"), vmem_limit_bytes=VMEM_LIMIT),
        name=name,
    )(a, w_in, w_in), n_lru + n_rw_t


def _merge_kernel(ga_ref, orw_ref, wpa_ref, wpb_ref, sa_ref, sb_ref, o_ref, wa_bf_ref, wb_bf_ref):
    @pl.when(pl.program_id(1) == 0)
    def _cast_weights():
        wa_bf_ref[...] = wpa_ref[...].astype(BF16)
        wb_bf_ref[...] = wpb_ref[...].astype(BF16)

    y_a = jnp.dot(ga_ref[...], wa_bf_ref[...], preferred_element_type=F32)
    y_b = jnp.dot(orw_ref[...], wb_bf_ref[...], preferred_element_type=F32)
    o_ref[...] = (_sigmoid(sa_ref[...]) * y_a + _sigmoid(sb_ref[...]) * y_b).astype(o_ref.dtype)


def _merge(ga, o_rw, w_pa, w_pb, p_all, gates_col, layer, *, tm, tn, name):
    m, ca = ga.shape
    cb = o_rw.shape[1]
    d = w_pa.shape[-1]
    tm = min(tm, m)
    tn = _pick_tile(math.gcd(d, gates_col), tn)
    off = gates_col // tn
    nd = d // tn
    assert m % tm == 0
    return pl.pallas_call(
        _merge_kernel,
        grid=(nd, m // tm),
        in_specs=[pl.BlockSpec((tm, ca), lambda j, i: (i, 0)),
                  pl.BlockSpec((tm, cb), lambda j, i: (i, 0)),
                  pl.BlockSpec((None, ca, tn), lambda j, i: (layer, 0, j)),
                  pl.BlockSpec((None, cb, tn), lambda j, i: (layer, 0, j)),
                  pl.BlockSpec((tm, tn), lambda j, i: (i, off + j)),
                  pl.BlockSpec((tm, tn), lambda j, i: (i, off + nd + j))],
        out_specs=pl.BlockSpec((tm, tn), lambda j, i: (i, j)),
        out_shape=jax.ShapeDtypeStruct((m, d), BF16),
        scratch_shapes=[pltpu.VMEM((ca, tn), BF16), pltpu.VMEM((cb, tn), BF16)],
        compiler_params=pltpu.CompilerParams(
            dimension_semantics=("arbitrary", "arbitrary"), vmem_limit_bytes=VMEM_LIMIT),
        name=name,
    )(ga, o_rw, w_pa, w_pb, p_all, p_all)


def _ffn_up_kernel(a_ref, wg_ref, wv_ref, cwg_ref, cwv_ref, cbg_ref, cbv_ref, bg_ref, bv_ref,
                   o_ref, nbg_ref, nbv_ref, wgb_ref, wvb_ref, xg_ref, xv_ref, *, bb, tt, nt, width):
    i = pl.program_id(1)
    tn = o_ref.shape[-1]
    base = SUBLANES - (width - 1)

    @pl.when(i == 0)
    def _cast_weights():
        wgb_ref[...] = wg_ref[...].astype(BF16)
        wvb_ref[...] = wv_ref[...].astype(BF16)

    @pl.when(i % nt == 0)
    def _sequence_start():
        xg_ref[:, base:SUBLANES, :] = bg_ref[...]
        xv_ref[:, base:SUBLANES, :] = bv_ref[...]

    a = a_ref[...]

    def half(wb_ref, x_ref, cw_ref, cb_ref, nb_ref):
        x_ref[:, SUBLANES:SUBLANES + tt, :] = jnp.dot(
            a, wb_ref[...], preferred_element_type=F32).reshape(bb, tt, tn)
        conv = cb_ref[...] + cw_ref[0:1, :] * x_ref[:, base:base + tt, :]
        for j in range(1, width):
            conv = conv + cw_ref[j:j + 1, :] * x_ref[:, base + j:base + j + tt, :]
        last = x_ref[:, tt + base:tt + SUBLANES, :]
        nb_ref[...] = last
        x_ref[:, base:SUBLANES, :] = last
        return conv

    cg = half(wgb_ref, xg_ref, cwg_ref, cbg_ref, nbg_ref)
    cv = half(wvb_ref, xv_ref, cwv_ref, cbv_ref, nbv_ref)
    o_ref[...] = (_silu(cg) * cv).reshape(bb * tt, tn).astype(o_ref.dtype)


def _ffn_up(a, w, cw, cb, buf, layer, *, t_len, tm, tn, name):
    m, d = a.shape
    f = w.shape[-1] // 2
    width = cw.shape[1]
    bsz = m // t_len
    tm = min(tm, m)
    tn = _pick_tile(f, tn)
    nf = f // tn
    if tm <= t_len:
        assert t_len % tm == 0
        bb, tt = 1, tm
    else:
        assert tm % t_len == 0 and t_len == SUBLANES
        bb, tt = tm // t_len, t_len
    nt = t_len // tt
    assert t_len >= width - 1
    wspec = lambda off: pl.BlockSpec((None, d, tn), lambda j, i: (layer, 0, j + off))
    cspec = lambda rows, off: pl.BlockSpec((None, rows, tn), lambda j, i: (layer, 0, j + off))
    bspec = lambda off: pl.BlockSpec((bb, width - 1, tn), lambda j, i: (i // nt, 0, j + off))
    return pl.pallas_call(
        functools.partial(_ffn_up_kernel, bb=bb, tt=tt, nt=nt, width=width),
        grid=(nf, m // tm),
        in_specs=[pl.BlockSpec((tm, d), lambda j, i: (i, 0)), wspec(0), wspec(nf),
                  cspec(width, 0), cspec(width, nf), cspec(1, 0), cspec(1, nf), bspec(0), bspec(nf)],
        out_specs=[pl.BlockSpec((tm, tn), lambda j, i: (i, j)), bspec(0), bspec(0)],
        out_shape=[jax.ShapeDtypeStruct((m, f), BF16),
                   jax.ShapeDtypeStruct((bsz, width - 1, f), F32),
                   jax.ShapeDtypeStruct((bsz, width - 1, f), F32)],
        scratch_shapes=[pltpu.VMEM((d, tn), BF16), pltpu.VMEM((d, tn), BF16),
                        pltpu.VMEM((bb, tt + SUBLANES, tn), F32), pltpu.VMEM((bb, tt + SUBLANES, tn), F32)],
        compiler_params=pltpu.CompilerParams(
            dimension_semantics=("arbitrary", "arbitrary"), vmem_limit_bytes=VMEM_LIMIT),
        name=name,
    )(a, w, w, cw, cw, cb, cb, buf, buf)


def _norm_mod_kernel(x_ref, g_ref, sc_ref, sh_ref, o_ref):
    x = x_ref[...]
    y = x * lax.rsqrt(jnp.mean(x * x, axis=-1, keepdims=True) + RMS_EPS) * g_ref[...]
    y = y * (1.0 + sc_ref[...]) + sh_ref[...]
    o_ref[...] = y.reshape(o_ref.shape).astype(o_ref.dtype)


def _norm_mod(x, g, layer, mod3, sc_idx, sh_idx, *, bb, tt, name):
    b, t, d = x.shape
    nt = t // tt
    return pl.pallas_call(
        _norm_mod_kernel,
        grid=(b // bb, nt),
        in_specs=[pl.BlockSpec((bb, tt, d), lambda i, j: (i, j, 0)),
                  pl.BlockSpec((None, 1, d), lambda i, j: (layer, 0, 0)),
                  pl.BlockSpec((bb, 1, d), lambda i, j: (i, 0, sc_idx)),
                  pl.BlockSpec((bb, 1, d), lambda i, j: (i, 0, sh_idx))],
        out_specs=pl.BlockSpec((bb * tt, d), lambda i, j: (i * nt + j, 0)),
        out_shape=jax.ShapeDtypeStruct((b * t, d), BF16),
        compiler_params=pltpu.CompilerParams(
            dimension_semantics=("arbitrary", "arbitrary"), vmem_limit_bytes=VMEM_LIMIT),
        name=name,
    )(x, g, mod3, mod3)


def _final_norm_kernel(x_ref, g_ref, o_ref):
    x = x_ref[...]
    o_ref[...] = x * lax.rsqrt(jnp.mean(x * x, axis=-1, keepdims=True) + RMS_EPS) * g_ref[...]


def _final_norm(x, g, *, bb, tt, name):
    b, t, d = x.shape
    return pl.pallas_call(
        _final_norm_kernel,
        grid=(b // bb, t // tt),
        in_specs=[pl.BlockSpec((bb, tt, d), lambda i, j: (i, j, 0)),
                  pl.BlockSpec((1, d), lambda i, j: (0, 0))],
        out_specs=pl.BlockSpec((bb, tt, d), lambda i, j: (i, j, 0)),
        out_shape=jax.ShapeDtypeStruct((b, t, d), F32),
        compiler_params=pltpu.CompilerParams(
            dimension_semantics=("arbitrary", "arbitrary"), vmem_limit_bytes=VMEM_LIMIT),
        name=name,
    )(x, g)


def _lru_kernel(x_ref, gate_ref, buf_ref, h0_ref, cw_ref, cb_ref, wa_ref, ba_ref, wi_ref, bi_ref,
                lam_ref, ga_ref, nbuf_ref, nh_ref, xx_ref, hc_ref, a_ref, b_ref, h_ref, *, bb, tt, width):
    c = x_ref.shape[-1]
    rows = bb * tt
    halo = width - 1
    base = SUBLANES - halo

    @pl.when(pl.program_id(1) == 0)
    def _init():
        xx_ref[:, base:SUBLANES, :] = buf_ref[...]
        hc_ref[...] = h0_ref[...]

    xx_ref[:, SUBLANES:SUBLANES + tt, :] = x_ref[...]
    conv = cb_ref[...] + cw_ref[0:1, :] * xx_ref[:, base:base + tt, :]
    for j in range(1, width):
        conv = conv + cw_ref[j:j + 1, :] * xx_ref[:, base + j:base + j + tt, :]
    last = xx_ref[:, tt + base:tt + SUBLANES, :]
    nbuf_ref[...] = last
    xx_ref[:, base:SUBLANES, :] = last

    xc = conv.reshape(rows, c)
    r = _sigmoid(_dot(xc, wa_ref[...]) + ba_ref[...])
    i = _sigmoid(_dot(xc, wi_ref[...]) + bi_ref[...])
    log_a = -LRU_C * r * _softplus(-lam_ref[...])
    a = jnp.exp(log_a)
    inp = jnp.sqrt(-jnp.tanh(log_a) * (a * a + 1.0)) * (i * xc)

    pos = lax.broadcasted_iota(jnp.int32, (rows, 1), 0) % SUBLANES
    for s in (1, 2, 4):
        a_sh = pltpu.roll(a, s, 0)
        b_sh = pltpu.roll(inp, s, 0)
        m = pos >= s
        inp = jnp.where(m, a * b_sh + inp, inp)
        a = jnp.where(m, a * a_sh, a)

    if tt == SUBLANES:
        h = (inp.reshape(bb, tt, c) + a.reshape(bb, tt, c) * hc_ref[...]).reshape(rows, c)
    else:
        a_ref[...] = a
        b_ref[...] = inp

        def group(g, carry):
            sl = pl.ds(pl.multiple_of(g * SUBLANES, SUBLANES), SUBLANES)
            hg = b_ref[sl, :] + a_ref[sl, :] * carry
            h_ref[sl, :] = hg
            return hg[SUBLANES - 1:SUBLANES, :]

        lax.fori_loop(0, rows // SUBLANES, group, hc_ref[0])
        h = h_ref[...]

    h3 = h.reshape(bb, tt, c)
    hc_ref[...] = h3[:, tt - 1:tt, :]
    nh_ref[...] = h3[:, tt - 1:tt, :]
    ga_ref[...] = (_gelu_tanh(gate_ref[...]) * h3).reshape(rows, c).astype(ga_ref.dtype)


def _lru_branch(p_lru, buf, h0, layer, P, *, bb, tt, name):
    b, t, _ = p_lru.shape
    c = h0.shape[-1]
    width = P["lru_conv_w"].shape[1]
    assert t >= width - 1 and (tt == SUBLANES or bb == 1)
    lw = lambda shape: pl.BlockSpec((None,) + shape, lambda i, j: (layer,) + (0,) * len(shape))
    kern = functools.partial(_lru_kernel, bb=bb, tt=tt, width=width)
    return pl.pallas_call(
        kern,
        grid=(b // bb, t // tt),
        in_specs=[pl.BlockSpec((bb, tt, c), lambda i, j: (i, j, 0)),
                  pl.BlockSpec((bb, tt, c), lambda i, j: (i, j, 1)),
                  pl.BlockSpec((bb, width - 1, c), lambda i, j: (i, 0, 0)),
                  pl.BlockSpec((bb, 1, c), lambda i, j: (i, 0, 0)),
                  lw((width, c)), lw((1, c)), lw((c, c)), lw((1, c)), lw((c, c)), lw((1, c)), lw((1, c))],
        out_specs=[pl.BlockSpec((bb * tt, c), lambda i, j: (i * (t // tt) + j, 0)),
                   pl.BlockSpec((bb, width - 1, c), lambda i, j: (i, 0, 0)),
                   pl.BlockSpec((bb, 1, c), lambda i, j: (i, 0, 0))],
        out_shape=[jax.ShapeDtypeStruct((b * t, c), BF16),
                   jax.ShapeDtypeStruct((b, width - 1, c), F32),
                   jax.ShapeDtypeStruct((b, 1, c), F32)],
        scratch_shapes=[pltpu.VMEM((bb, tt + SUBLANES, c), F32),
                        pltpu.VMEM((bb, 1, c), F32),
                        pltpu.VMEM((bb * tt, c), F32),
                        pltpu.VMEM((bb * tt, c), F32),
                        pltpu.VMEM((bb * tt, c), F32)],
        compiler_params=pltpu.CompilerParams(
            dimension_semantics=("arbitrary", "arbitrary"), vmem_limit_bytes=VMEM_LIMIT),
        name=name,
    )(p_lru, p_lru, buf, h0, P["lru_conv_w"], P["lru_conv_b"], P["lru_wa_bd"], P["lru_ba"],
      P["lru_wi_bd"], P["lru_bi"], P["lru_lambda"])


def _rwkv_kernel(xa_ref, xb_ref, sbuf_ref, s0_ref, mu_ref, w0_ref, w2_ref, a0_ref, a2_ref, g2_ref, kkp_ref,
                 ka_ref, rk_ref, gng_ref, gnb_ref, seg_ref, segt_ref, ns_all_ref,
                 o_ref, nshift_ref, ns_ref,
                 sbd_ref, prev_ref, r_ref, kk_ref, km_ref, b_ref, v_ref, lw_ref, y_ref, tmp_ref,
                 *, bb, tt, chunk, dr):
    del ns_all_ref
    rows = bb * tt
    nrw = sbuf_ref.shape[-1]
    pairs = dr // LANES
    gb = RW_HEAD // chunk
    ti = pl.program_id(1)

    @pl.when(ti == 0)
    def _init():
        prev_ref[...] = sbuf_ref[...]
        tmp_ref[...] = jnp.zeros_like(tmp_ref)

        def pack(bi, carry):
            for j in range(pairs):
                sbd_ref[bi, j, 0:RW_HEAD, 0:RW_HEAD] = s0_ref[bi, 2 * j]
                sbd_ref[bi, j, 0:RW_HEAD, RW_HEAD:LANES] = jnp.zeros((RW_HEAD, RW_HEAD), F32)
                tmp_ref[:, 0:RW_HEAD] = s0_ref[bi, 2 * j + 1]
                sbd_ref[bi, j, RW_HEAD:ROWS, :] = pltpu.roll(tmp_ref[...], RW_HEAD, 1)
            return carry

        lax.fori_loop(0, bb, pack, 0)

    x3 = jnp.concatenate([xa_ref[...], xb_ref[:, :, 0:nrw - xa_ref.shape[-1]]], axis=-1)
    rolled = pltpu.roll(x3.reshape(rows, nrw), 1, 0).reshape(bb, tt, nrw)
    t_pos = lax.broadcasted_iota(jnp.int32, (bb, tt, 1), 1)
    prev3 = jnp.where(t_pos == 0, prev_ref[...], rolled)
    last = x3[:, tt - 1:tt, :]
    prev_ref[...] = last
    nshift_ref[...] = last
    xs = (x3 + (prev3 - x3) * mu_ref[...]).reshape(rows, nrw)

    r = xs[:, 0:dr]
    k = xs[:, dr:2 * dr]
    v = xs[:, 2 * dr:3 * dr]
    lora = xs[:, 3 * dr:]
    w_log = -_softplus(-(w0_ref[...] + _dot(jnp.tanh(lora), w2_ref[...]))) - 0.5
    a = _sigmoid(a0_ref[...] + _dot(lora, a2_ref[...]))
    gg = _dot(_sigmoid(lora), g2_ref[...])

    seg = seg_ref[...]
    segt = segt_ref[...]
    head_sum = lambda z: _dot_exact_rhs(z, seg, 3)
    head_bcast = lambda z: _dot_exact_rhs(z, segt, 2)

    kk = k * kkp_ref[...]
    kk = kk * head_bcast(1.0 / jnp.maximum(jnp.sqrt(head_sum(kk * kk)), 1e-12))
    km = k * (1.0 + (a - 1.0) * ka_ref[...])
    r_ref[...] = r
    kk_ref[...] = kk
    km_ref[...] = km
    b_ref[...] = kk * a
    v_ref[...] = v
    lw_ref[...] = -jnp.exp(w_log)

    ri = lax.broadcasted_iota(jnp.int32, (ROWS, ROWS), 0)
    ci = lax.broadcasted_iota(jnp.int32, (ROWS, ROWS), 1)
    same = (ri // chunk) == (ci // chunk)
    m_strict = same & ((ci % chunk) < (ri % chunk))
    m_incl = same & ((ci % chunk) <= (ri % chunk))
    m_pair = (ri // RW_HEAD) == (ci // RW_HEAD)
    eye = jnp.where(ri == ci, 1.0, 0.0)
    lvl_masks = []
    s = 1
    while s < chunk:
        lvl_masks.append(((ri // (2 * s)) == (ci // (2 * s))) & ((ri // s) != (ci // s)))
        s *= 2
    r64 = lax.broadcasted_iota(jnp.int32, (RW_HEAD, RW_HEAD), 0)
    c64 = lax.broadcasted_iota(jnp.int32, (RW_HEAD, RW_HEAD), 1)
    tril = jnp.where(((r64 // chunk) == (c64 // chunk)) & (c64 <= r64), 1.0, 0.0).astype(BF16)
    lane_lo = lax.broadcasted_iota(jnp.int32, (1, 1, LANES), 2) < RW_HEAD

    def stack_par(z3):
        return jnp.concatenate([jnp.where(lane_lo, z3, 0.0), jnp.where(lane_lo, 0.0, z3)],
                               axis=1).reshape(ROWS, LANES)

    def stack_dup(z3):
        return jnp.concatenate([z3, z3], axis=1).reshape(ROWS, LANES)

    def row_chunk(rc, carry):
        sl = pl.ds(pl.multiple_of(rc * RW_HEAD, RW_HEAD), RW_HEAD)
        lw = lw_ref[sl, :]
        c_in = _dot_exact_lhs(tril, lw, 3)
        p_in = jnp.exp(c_in)
        p_inv = jnp.exp(-c_in)
        qt = kk_ref[sl, :] * jnp.exp(c_in - lw)
        rt = r_ref[sl, :] * p_in
        kt = km_ref[sl, :] * p_inv
        bt = b_ref[sl, :] * p_inv
        vv = v_ref[sl, :]
        p_end = p_in.reshape(gb, chunk, dr)[:, chunk - 1:chunk, :]
        pr = range(pairs)
        lanes = [slice(j * LANES, (j + 1) * LANES) for j in pr]
        split3 = lambda z: [z[:, ls].reshape(gb, chunk, LANES) for ls in lanes]
        q3, r3, k3, b3, v3 = split3(qt), split3(rt), split3(kt), split3(bt), split3(vv)
        lq = [stack_par(z) for z in q3]
        lr = [stack_par(z) for z in r3]
        vm = [stack_par(z) for z in v3]
        g = [_dot_nt(jnp.concatenate([lq[j], lr[j]], axis=0),
                     jnp.concatenate([stack_dup(b3[j]), stack_dup(k3[j])], axis=0)) for j in pr]
        l_b = [jnp.where(m_strict, g[j][0:ROWS, 0:ROWS], 0.0) for j in pr]
        m_k = [jnp.where(m_strict, g[j][0:ROWS, ROWS:], 0.0) for j in pr]
        n_kb = [jnp.concatenate([jnp.where(m_incl, g[j][ROWS:, ROWS:], 0.0),
                                 jnp.where(m_incl, -g[j][ROWS:, 0:ROWS], 0.0)], axis=1) for j in pr]
        mkv = [_dot(m_k[j], vm[j]) for j in pr]
        t_inv = [eye - jnp.where(lvl_masks[0], l_b[j], 0.0) for j in pr]
        for lm in lvl_masks[1:]:
            w = [_dot_hi(t_inv[j], jnp.where(lm, l_b[j], 0.0)) for j in pr]
            t_inv = [t_inv[j] - _dot_hi(w[j], t_inv[j]) for j in pr]
        bis = [0 if bb == 1 else (rc * RW_HEAD + s * chunk) // tt for s in range(gb)]
        st = [[sbd_ref[bis[s], j] for s in range(gb)] for j in pr]
        qa, ra = [], []
        for j in pr:
            lq3 = lq[j].reshape(gb, 2 * chunk, LANES)
            lr3 = lr[j].reshape(gb, 2 * chunk, LANES)
            qr = [_dot_nt(jnp.concatenate([lq3[s], lr3[s]], axis=0), st[j][s]) for s in range(gb)]
            qa.append(jnp.concatenate([z[0:2 * chunk] for z in qr], axis=0) if gb > 1 else qr[0][0:2 * chunk])
            ra.append(jnp.concatenate([z[2 * chunk:] for z in qr], axis=0) if gb > 1 else qr[0][2 * chunk:])
        x = [_dot_hi(t_inv[j], qa[j] + mkv[j]) for j in pr]
        ys = [ra[j] + _dot(n_kb[j], jnp.concatenate([vm[j], x[j]], axis=0)) for j in pr]
        for j in pr:
            ys3 = ys[j].reshape(gb, 2 * chunk, LANES)
            y_ref[sl, lanes[j]] = (ys3[:, 0:chunk] + ys3[:, chunk:]).reshape(RW_HEAD, LANES)
        for j in pr:
            u3 = x[j].reshape(gb, 2 * chunk, LANES)
            u3 = u3[:, 0:chunk] + u3[:, chunk:]
            for s in range(gb):
                ds = _dot_tn(jnp.concatenate([v3[j][s], -u3[s]], axis=0),
                             jnp.concatenate([k3[j][s], b3[j][s]], axis=0))
                sbd_ref[bis[s], j] = p_end[s][:, lanes[j]] * (st[j][s] + jnp.where(m_pair, ds, 0.0))
        return carry

    lax.fori_loop(0, rows // RW_HEAD, row_chunk, 0)

    y = y_ref[...]
    inv_n = 1.0 / RW_HEAD
    d = y - head_bcast(head_sum(y) * inv_n)
    rstd = lax.rsqrt(head_sum(d * d) * inv_n + GN_EPS)
    gn = d * head_bcast(rstd) * gng_ref[...] + gnb_ref[...]
    bonus = head_bcast(head_sum(r_ref[...] * km_ref[...] * rk_ref[...])) * v_ref[...]
    o_ref[...] = ((gn + bonus) * gg).astype(o_ref.dtype)

    @pl.when(ti == pl.num_programs(1) - 1)
    def _finish():
        def unpack(bi, carry):
            for j in range(pairs):
                ns_ref[bi, 2 * j] = sbd_ref[bi, j, 0:RW_HEAD, 0:RW_HEAD]
                ns_ref[bi, 2 * j + 1] = pltpu.roll(sbd_ref[bi, j, RW_HEAD:ROWS, :], RW_HEAD, 1)[:, 0:RW_HEAD]
            return carry

        lax.fori_loop(0, bb, unpack, 0)


def _rwkv_branch(p_all, rw_col, sbuf, s0, ns_all, layer, P, *, bb, tt, chunk, name):
    b, t, _ = p_all.shape
    nrw = sbuf.shape[-1]
    heads = s0.shape[1]
    dr = heads * RW_HEAD
    nl = nrw - 3 * dr
    rows = bb * tt
    xw = rw_col
    assert xw % LANES == 0 and xw < nrw <= 2 * xw and rw_col + 2 * xw <= p_all.shape[-1]
    assert dr % LANES == 0 and RW_HEAD % chunk == 0 and tt % chunk == 0 and rows % RW_HEAD == 0
    assert tt == chunk or bb == 1
    lw = lambda shape: pl.BlockSpec((None,) + shape, lambda i, j: (layer,) + (0,) * len(shape))
    kern = functools.partial(_rwkv_kernel, bb=bb, tt=tt, chunk=chunk, dr=dr)
    big = lambda: pltpu.VMEM((rows, dr), F32)
    args = (p_all, p_all, sbuf, s0, P["rw_mu_p"], P["rw_w0"], P["rw_w2_p"], P["rw_a0"], P["rw_a2_p"], P["rw_g2_p"],
            P["rw_kk"], P["rw_ka"], P["rw_rk"], P["rw_gn_g"], P["rw_gn_b"], P["seg"], P["segt"], ns_all)
    return pl.pallas_call(
        kern,
        grid=(b // bb, t // tt),
        in_specs=[pl.BlockSpec((bb, tt, xw), lambda i, j: (i, j, 1)),
                  pl.BlockSpec((bb, tt, xw), lambda i, j: (i, j, 2)),
                  pl.BlockSpec((bb, 1, nrw), lambda i, j: (i, 0, 0)),
                  pl.BlockSpec((bb, heads, RW_HEAD, RW_HEAD), lambda i, j: (i, 0, 0, 0)),
                  lw((1, nrw)), lw((1, dr)), lw((nl, dr)), lw((1, dr)), lw((nl, dr)), lw((nl, dr)),
                  lw((1, dr)), lw((1, dr)), lw((1, dr)), lw((1, dr)), lw((1, dr)),
                  pl.BlockSpec((dr, LANES), lambda i, j: (0, 0)),
                  pl.BlockSpec((LANES, dr), lambda i, j: (0, 0)),
                  pl.BlockSpec(memory_space=pl.ANY)],
        out_specs=[pl.BlockSpec((rows, dr), lambda i, j: (i * (t // tt) + j, 0)),
                   pl.BlockSpec((bb, 1, nrw), lambda i, j: (i, 0, 0)),
                   pl.BlockSpec((None, bb, heads, RW_HEAD, RW_HEAD), lambda i, j: (layer, i, 0, 0, 0))],
        out_shape=[jax.ShapeDtypeStruct((b * t, dr), BF16),
                   jax.ShapeDtypeStruct((b, 1, nrw), F32),
                   jax.ShapeDtypeStruct(ns_all.shape, F32)],
        input_output_aliases={len(args) - 1: 2},
        scratch_shapes=[pltpu.VMEM((bb, dr // LANES, ROWS, LANES), F32),
                        pltpu.VMEM((bb, 1, nrw), F32),
                        big(), big(), big(), big(), big(), big(), big(),
                        pltpu.VMEM((RW_HEAD, LANES), F32)],
        compiler_params=pltpu.CompilerParams(
            dimension_semantics=("arbitrary", "arbitrary"), vmem_limit_bytes=VMEM_LIMIT),
        name=name,
    )(*args)


def _trunk(x, mod, states, P, cfg, tag):
    bsz, t_len, d = x.shape
    m = bsz * t_len
    depth = mod.shape[0]
    lru_conv, lru_h, rw_shift, rw_s, ffn_conv = states
    dl = lru_h.shape[-1]
    nrw = P["rw_mu_p"].shape[-1]
    nrw0 = rw_shift.shape[-1]
    outs = ([], [], [], [])
    tm, tn = cfg["tm"], cfg["tn"]
    x = x.reshape(m, d)
    ns_all = jnp.zeros(rw_s.shape, F32)
    for l in range(depth):
        mod3 = mod[l][:, None, :]
        if t_len % min(tm, m) == 0:
            gate1, gate2 = (mod3, 2), (mod3, 5)
        else:
            gate1 = (jnp.repeat(mod[l][:, 2 * d:3 * d], t_len, axis=0), 0)
            gate2 = (jnp.repeat(mod[l][:, 5 * d:6 * d], t_len, axis=0), 0)
        norm = functools.partial(_norm_mod, bb=cfg["nbb"], tt=cfg["ntt"])

        h = norm(x.reshape(bsz, t_len, d), P["norm_mix"], l, mod3, 1, 0, name=f"norm_mix_{tag}")
        p_all, gates_col = _in_proj(h, P["w_in"], l, n_lru=2 * dl, n_rw=nrw, n_gates=2 * d, tm=tm, tn=tn,
                                    name=f"in_proj_{tag}")
        p_all3 = p_all.reshape(bsz, t_len, -1)

        ga, n_lru_buf, n_lru_h = _lru_branch(p_all3, lru_conv[l], lru_h[l][:, None, :], l, P,
                                             bb=cfg["lbb"], tt=cfg["ltt"], name=f"lru_{tag}")
        sbuf = jnp.pad(rw_shift[l], ((0, 0), (0, nrw - nrw0)))[:, None, :]
        o_rw, n_shift, ns_all = _rwkv_branch(p_all3, 2 * dl, sbuf, rw_s[l], ns_all, l, P, bb=cfg["rbb"],
                                             tt=cfg["rtt"], chunk=cfg["chunk"], name=f"rwkv_{tag}")
        merged = _merge(ga, o_rw, P["w_pa"], P["w_pb"], p_all, gates_col, l, tm=tm, tn=cfg["tn_res"],
                        name=f"merge_{tag}")
        x = _matmul(merged, P["w_o"], l, tm=tm, tn=cfg["tn_res"], resid=(x,) + gate1 + (t_len,), name=f"o_{tag}")

        h2 = norm(x.reshape(bsz, t_len, d), P["norm_ffn"], l, mod3, 4, 3, name=f"norm_ffn_{tag}")
        act, n_buf_g, n_buf_v = _ffn_up(h2, P["w_up"], P["ffn_conv_w"], P["ffn_conv_b"], ffn_conv[l], l,
                                        t_len=t_len, tm=tm, tn=cfg["tn_up"], name=f"ffn_up_{tag}")
        x = _matmul(act, P["w_down"], l, tm=cfg["tm_down"], tn=cfg["tn_res"], resid=(x,) + gate2 + (t_len,),
                    name=f"ffn_down_{tag}")

        n_ffn_buf = jnp.concatenate([n_buf_g, n_buf_v], axis=-1)
        for lst, ns in zip(outs, (n_lru_buf, n_lru_h[:, 0, :], n_shift[:, 0, :nrw0], n_ffn_buf)):
            lst.append(ns)
    y = _final_norm(x.reshape(bsz, t_len, d), P["norm_final"], bb=cfg["nbb"], tt=cfg["ntt"], name=f"norm_final_{tag}")
    st = [jnp.stack(lst, axis=0) for lst in outs]
    return y, (st[0], st[1], st[2], ns_all, st[3])


def kernel(x_prompt, x_sample, c_prompt, c_sample, state_lru_conv, state_lru_h, state_rwkv_shift,
           state_rwkv_S, state_ffn_conv, w_ada, b_ada, norm_mix, norm_ffn, w_in, lru_conv_w,
           lru_conv_b, lru_wa, lru_ba, lru_wi, lru_bi, lru_lambda, w_pa, rw_mu, rw_w0, rw_w2, rw_a0,
           rw_a2, rw_g2, rw_kk, rw_ka, rw_rk, rw_gn_g, rw_gn_b, w_pb, w_o, w_up, ffn_conv_w,
           ffn_conv_b, w_down, norm_final):
    depth, d, _ = w_ada.shape
    bp, tp, _ = x_prompt.shape
    bs, ts, _ = x_sample.shape
    dl = lru_lambda.shape[-1]
    nblk, blk = lru_wa.shape[1], lru_wa.shape[2]
    heads = rw_rk.shape[1]
    dr = heads * RW_HEAD
    nrw0 = rw_mu.shape[-1]
    nl = _cdiv(nrw0 - 3 * dr, LANES) * LANES
    nrw = 3 * dr + nl
    lw_n, la_n = rw_w2.shape[1], rw_a2.shape[1]

    row = lambda p: p[:, None, :]
    eye = jnp.eye(nblk, dtype=F32)
    block_diag = lambda w: (eye[:, None, :, None] * w[:, :, :, None, :]).reshape(depth, dl, dl).astype(BF16)
    pad_rows = lambda w, off: jnp.pad(w, ((0, 0), (off, nl - off - w.shape[1]), (0, 0))).astype(BF16)
    head_of = jnp.arange(dr) // RW_HEAD
    seg = (head_of[:, None] == jnp.arange(LANES)[None, :]).astype(BF16)
    P = dict(
        norm_mix=row(norm_mix), norm_ffn=row(norm_ffn), norm_final=norm_final[None, :],
        w_in=w_in, lru_conv_w=lru_conv_w, lru_conv_b=row(lru_conv_b), lru_wa_bd=block_diag(lru_wa),
        lru_ba=row(lru_ba), lru_wi_bd=block_diag(lru_wi), lru_bi=row(lru_bi), lru_lambda=row(lru_lambda),
        w_pa=w_pa, w_pb=w_pb, w_o=w_o, w_up=w_up, w_down=w_down,
        rw_mu_p=row(jnp.pad(rw_mu, ((0, 0), (0, nrw - nrw0)))),
        rw_w0=row(rw_w0), rw_a0=row(rw_a0),
        rw_w2_p=pad_rows(rw_w2, 0), rw_a2_p=pad_rows(rw_a2, lw_n), rw_g2_p=pad_rows(rw_g2, lw_n + la_n),
        rw_kk=row(rw_kk), rw_ka=row(rw_ka), rw_rk=rw_rk.reshape(depth, 1, dr),
        rw_gn_g=row(rw_gn_g), rw_gn_b=row(rw_gn_b), seg=seg, segt=seg.T,
        ffn_conv_w=ffn_conv_w, ffn_conv_b=row(ffn_conv_b),
    )

    nb = bp + bs
    nb_pad = _cdiv(nb, SUBLANES) * SUBLANES
    c_all = jnp.concatenate([c_prompt, c_sample, jnp.zeros((nb_pad - nb, d), F32)], axis=0)
    mods = [_matmul(c_all, w_ada, l, tm=nb_pad, tn=1024, bias=row(b_ada), act="silu", name="adaln")
            for l in range(depth)]
    mod = jnp.stack(mods, axis=0)

    zeros = lambda *s: jnp.zeros((depth, bp) + s, F32)
    p_states = (zeros(lru_conv_w.shape[1] - 1, dl), zeros(dl), zeros(nrw0),
                zeros(heads, RW_HEAD, RW_HEAD), zeros(ffn_conv_w.shape[1] - 1, w_up.shape[-1]))
    s_states = (state_lru_conv, state_lru_h, state_rwkv_shift, state_rwkv_S, state_ffn_conv)

    tiles = dict(tm=1024, tn=1024, tn_res=512, tn_up=512, tm_down=512)
    cfg_p = dict(tiles, nbb=1, ntt=min(tp, 512), lbb=1, ltt=min(tp, 512),
                 rbb=1, rtt=min(tp, 256), chunk=min(tp, RW_HEAD))
    sb = min(bs, RW_HEAD // ts) if ts == SUBLANES else 1
    cfg_s = dict(tiles, nbb=min(bs, 64), ntt=ts, lbb=min(bs, 32), ltt=ts, rbb=sb, rtt=ts, chunk=ts)
    y_p, st_p = _trunk(x_prompt, mod[:, :bp], p_states, P, cfg_p, "prompt")
    y_s, st_s = _trunk(x_sample, mod[:, bp:nb], s_states, P, cfg_s, "sample")
    return (y_p, y_s) + st_p + st_s
```

```python
import functools
import math

import jax
import jax.numpy as jnp
from jax import lax
from jax.experimental import pallas as pl
from jax.experimental.pallas import tpu as pltpu

F32 = jnp.float32
BF16 = jnp.bfloat16

LANES = 128
SUBLANES = 8
RW_HEAD = 64
ROWS = 2 * RW_HEAD
LRU_C = 8.0
RMS_EPS = 1e-6
GN_EPS = 64e-5
VMEM_LIMIT = 56 * 1024 * 1024


def _cdiv(a, b):
    return -(-a // b)


def _dot(a, b):
    return jnp.dot(a.astype(BF16), b.astype(BF16), preferred_element_type=F32)


def _dot_nt(a, b):
    return lax.dot_general(a.astype(BF16), b.astype(BF16), (((1,), (1,)), ((), ())),
                           preferred_element_type=F32)


def _dot_tn(a, b):
    return lax.dot_general(a.astype(BF16), b.astype(BF16), (((0,), (0,)), ((), ())),
                           preferred_element_type=F32)


def _dot_hi(a, b):
    n = b.shape[1]
    a_hi, a_lo = _split(a, 2)
    b_hi, b_lo = _split(b, 2)
    lhs = jnp.concatenate([a_hi, a_lo], axis=1)
    rhs = jnp.concatenate([jnp.concatenate([b_hi, b_lo], axis=1),
                           jnp.concatenate([b_hi, jnp.zeros_like(b_lo)], axis=1)], axis=0)
    out = jnp.dot(lhs, rhs, preferred_element_type=F32)
    return out[:, :n] + out[:, n:]


def _split(x, n):
    parts = []
    for _ in range(n - 1):
        p = x.astype(BF16)
        parts.append(p)
        x = x - p.astype(F32)
    parts.append(x.astype(BF16))
    return parts


def _dot_exact_rhs(x, m, n=3):
    return sum(jnp.dot(p, m, preferred_element_type=F32) for p in _split(x, n))


def _dot_exact_lhs(m, x, n=3):
    return sum(jnp.dot(m, p, preferred_element_type=F32) for p in _split(x, n))


def _sigmoid(x):
    return 0.5 * jnp.tanh(0.5 * x) + 0.5


def _softplus(x):
    return jnp.maximum(x, 0.0) + jnp.log1p(jnp.exp(-jnp.abs(x)))


def _silu(x):
    return x * _sigmoid(x)


def _gelu_tanh(x):
    return 0.5 * x * (1.0 + jnp.tanh(math.sqrt(2.0 / math.pi) * (x + 0.044715 * (x * x * x))))


def _mm_resid_kernel(a_ref, w_ref, x_ref, gt_ref, o_ref, wbf_ref):
    @pl.when(pl.program_id(1) == 0)
    def _cast_weights():
        wbf_ref[...] = w_ref[...].astype(BF16)

    acc = jnp.dot(a_ref[...], wbf_ref[...], preferred_element_type=F32)
    nseq = gt_ref.shape[0]
    tm, tn = acc.shape
    out = x_ref[...].reshape(nseq, tm // nseq, tn) + gt_ref[...] * acc.reshape(nseq, tm // nseq, tn)
    o_ref[...] = out.reshape(tm, tn)


def _pick_tile(n, target):
    assert n % LANES == 0
    units = n // LANES
    best = max(u for u in range(1, units + 1) if units % u == 0 and u * LANES <= max(target, LANES))
    return best * LANES


def _gate_spec(layer, gate_col, boff, t_len, tm, tn, d):
    col = lambda j: gate_col * (d // tn) + j
    if tm <= t_len:
        assert t_len % tm == 0
        nt = t_len // tm
        return pl.BlockSpec((None, 1, 1, tn), lambda j, i: (layer, boff + i // nt, 0, col(j)))
    assert tm % t_len == 0 and boff % (tm // t_len) == 0
    nseq = tm // t_len
    return pl.BlockSpec((None, nseq, 1, tn), lambda j, i: (layer, boff // nseq + i, 0, col(j)))


def _matmul_resid(a, w, layer, x, mod, gate_col, boff, t_len, *, tm, tn, name):
    m, k = a.shape
    n = w.shape[-1]
    tm = min(tm, m)
    tn = _pick_tile(n, tn)
    assert m % tm == 0
    return pl.pallas_call(
        _mm_resid_kernel,
        grid=(n // tn, m // tm),
        in_specs=[pl.BlockSpec((tm, k), lambda j, i: (i, 0)),
                  pl.BlockSpec((None, k, tn), lambda j, i: (layer, 0, j)),
                  pl.BlockSpec((tm, tn), lambda j, i: (i, j)),
                  _gate_spec(layer, gate_col, boff, t_len, tm, tn, n)],
        out_specs=pl.BlockSpec((tm, tn), lambda j, i: (i, j)),
        out_shape=jax.ShapeDtypeStruct((m, n), F32),
        scratch_shapes=[pltpu.VMEM((k, tn), BF16)],
        compiler_params=pltpu.CompilerParams(
            dimension_semantics=("arbitrary", "arbitrary"), vmem_limit_bytes=VMEM_LIMIT),
        name=name,
    )(a, w, x, mod)


def _in_proj_kernel(a_ref, wa_ref, wb_ref, o_ref, wbf_ref, *, n_plain, shift):
    j = pl.program_id(0)
    first_row_tile = pl.program_id(1) == 0
    tn = wa_ref.shape[1]

    @pl.when(first_row_tile & (j < n_plain))
    def _cast_weights():
        wbf_ref[...] = wa_ref[...].astype(BF16)

    @pl.when(first_row_tile & (j >= n_plain))
    def _cast_shifted_weights():
        if shift == 0:
            wbf_ref[...] = wa_ref[...].astype(BF16)
            return
        k = wa_ref.shape[0]
        rc = math.gcd(k, 256)
        reps = tn // wb_ref.shape[1]
        lane = lax.broadcasted_iota(jnp.int32, (1, tn), 1)

        def rows(c, carry):
            sl = pl.ds(pl.multiple_of(c * rc, rc), rc)
            wb = wb_ref[sl, :]
            wb = jnp.concatenate([wb] * reps, axis=1) if reps > 1 else wb
            w = jnp.where(lane < tn - shift, pltpu.roll(wa_ref[sl, :], tn - shift, 1),
                          pltpu.roll(wb, tn - shift, 1))
            wbf_ref[sl, :] = w.astype(BF16)
            return carry

        lax.fori_loop(0, k // rc, rows, 0)

    o_ref[...] = jnp.dot(a_ref[...], wbf_ref[...], preferred_element_type=F32)


def _in_proj(a, w_in, layer, *, n_lru, n_rw, n_gates, tm, tn, name):
    m, k = a.shape
    n_in = w_in.shape[-1]
    tm = min(tm, m)
    tn = _pick_tile(math.gcd(n_lru, n_gates), tn)
    n_rw_t = _cdiv(n_rw, tn) * tn
    gates_start = n_in - n_gates
    assert m % tm == 0 and n_lru + n_rw_t <= n_in
    n_plain = (n_lru + n_rw_t) // tn
    base, shift = gates_start // tn, gates_start % tn
    wbw = tn // 2 if (tn // 2) % LANES == 0 and shift <= tn // 2 else tn
    wa_idx = lambda j: jnp.where(j < n_plain, j, j - n_plain + base)
    wb_idx = lambda j: (jnp.maximum(j - n_plain, 0) + base + (1 if shift else 0)) * (tn // wbw)
    n_out = n_lru + n_rw_t + n_gates
    return pl.pallas_call(
        functools.partial(_in_proj_kernel, n_plain=n_plain, shift=shift),
        grid=(n_out // tn, m // tm),
        in_specs=[pl.BlockSpec((tm, k), lambda j, i: (i, 0)),
                  pl.BlockSpec((None, k, tn), lambda j, i: (layer, 0, wa_idx(j))),
                  pl.BlockSpec((None, k, wbw), lambda j, i: (layer, 0, wb_idx(j)))],
        out_specs=pl.BlockSpec((tm, tn), lambda j, i: (i, j)),
        out_shape=jax.ShapeDtypeStruct((m, n_out), F32),
        scratch_shapes=[pltpu.VMEM((k, tn), BF16)],
        compiler_params=pltpu.CompilerParams(
            dimension_semantics=("arbitrary", "arbitrary"), vmem_limit_bytes=VMEM_LIMIT),
        name=name,
    )(a, w_in, w_in), n_lru + n_rw_t


def _merge_kernel(ga_ref, orw_ref, wpa_ref, wpb_ref, sa_ref, sb_ref, o_ref, wa_bf_ref, wb_bf_ref):
    @pl.when(pl.program_id(1) == 0)
    def _cast_weights():
        wa_bf_ref[...] = wpa_ref[...].astype(BF16)
        wb_bf_ref[...] = wpb_ref[...].astype(BF16)

    y_a = jnp.dot(ga_ref[...], wa_bf_ref[...], preferred_element_type=F32)
    y_b = jnp.dot(orw_ref[...], wb_bf_ref[...], preferred_element_type=F32)
    o_ref[...] = (_sigmoid(sa_ref[...]) * y_a + _sigmoid(sb_ref[...]) * y_b).astype(o_ref.dtype)


def _merge(ga, o_rw, w_pa, w_pb, p_all, gates_col, layer, *, tm, tn, name):
    m, ca = ga.shape
    cb = o_rw.shape[1]
    d = w_pa.shape[-1]
    tm = min(tm, m)
    tn = _pick_tile(math.gcd(d, gates_col), tn)
    off = gates_col // tn
    nd = d // tn
    assert m % tm == 0
    return pl.pallas_call(
        _merge_kernel,
        grid=(nd, m // tm),
        in_specs=[pl.BlockSpec((tm, ca), lambda j, i: (i, 0)),
                  pl.BlockSpec((tm, cb), lambda j, i: (i, 0)),
                  pl.BlockSpec((None, ca, tn), lambda j, i: (layer, 0, j)),
                  pl.BlockSpec((None, cb, tn), lambda j, i: (layer, 0, j)),
                  pl.BlockSpec((tm, tn), lambda j, i: (i, off + j)),
                  pl.BlockSpec((tm, tn), lambda j, i: (i, off + nd + j))],
        out_specs=pl.BlockSpec((tm, tn), lambda j, i: (i, j)),
        out_shape=jax.ShapeDtypeStruct((m, d), BF16),
        scratch_shapes=[pltpu.VMEM((ca, tn), BF16), pltpu.VMEM((cb, tn), BF16)],
        compiler_params=pltpu.CompilerParams(
            dimension_semantics=("arbitrary", "arbitrary"), vmem_limit_bytes=VMEM_LIMIT),
        name=name,
    )(ga, o_rw, w_pa, w_pb, p_all, p_all)


def _ffn_up_kernel(a_ref, wg_ref, wv_ref, cwg_ref, cwv_ref, cbg_ref, cbv_ref, bg_ref, bv_ref,
                   o_ref, nbg_ref, nbv_ref, wgb_ref, wvb_ref, xg_ref, xv_ref, *, bb, tt, nt, width):
    i = pl.program_id(1)
    tn = o_ref.shape[-1]
    base = SUBLANES - (width - 1)

    @pl.when(i == 0)
    def _cast_weights():
        wgb_ref[...] = wg_ref[...].astype(BF16)
        wvb_ref[...] = wv_ref[...].astype(BF16)

    @pl.when(i % nt == 0)
    def _sequence_start():
        xg_ref[:, base:SUBLANES, :] = bg_ref[...]
        xv_ref[:, base:SUBLANES, :] = bv_ref[...]

    a = a_ref[...]

    def half(wb_ref, x_ref, cw_ref, cb_ref, nb_ref):
        x_ref[:, SUBLANES:SUBLANES + tt, :] = jnp.dot(
            a, wb_ref[...], preferred_element_type=F32).reshape(bb, tt, tn)
        conv = cb_ref[...] + cw_ref[0:1, :] * x_ref[:, base:base + tt, :]
        for j in range(1, width):
            conv = conv + cw_ref[j:j + 1, :] * x_ref[:, base + j:base + j + tt, :]
        last = x_ref[:, tt + base:tt + SUBLANES, :]
        nb_ref[...] = last
        x_ref[:, base:SUBLANES, :] = last
        return conv

    cg = half(wgb_ref, xg_ref, cwg_ref, cbg_ref, nbg_ref)
    cv = half(wvb_ref, xv_ref, cwv_ref, cbv_ref, nbv_ref)
    o_ref[...] = (_silu(cg) * cv).reshape(bb * tt, tn).astype(o_ref.dtype)


def _ffn_up(a, w, cw, cb, buf, layer, *, t_len, tm, tn, name):
    m, d = a.shape
    f = w.shape[-1] // 2
    width = cw.shape[1]
    bsz = m // t_len
    tm = min(tm, m)
    tn = _pick_tile(f, tn)
    nf = f // tn
    if tm <= t_len:
        assert t_len % tm == 0
        bb, tt = 1, tm
    else:
        assert tm % t_len == 0 and t_len == SUBLANES
        bb, tt = tm // t_len, t_len
    nt = t_len // tt
    assert t_len >= width - 1
    wspec = lambda off: pl.BlockSpec((None, d, tn), lambda j, i: (layer, 0, j + off))
    cspec = lambda rows, off: pl.BlockSpec((None, rows, tn), lambda j, i: (layer, 0, j + off))
    bspec = lambda off: pl.BlockSpec((None, bb, width - 1, tn), lambda j, i: (layer, i // nt, 0, j + off))
    ospec = pl.BlockSpec((bb, width - 1, tn), lambda j, i: (i // nt, 0, j))
    return pl.pallas_call(
        functools.partial(_ffn_up_kernel, bb=bb, tt=tt, nt=nt, width=width),
        grid=(nf, m // tm),
        in_specs=[pl.BlockSpec((tm, d), lambda j, i: (i, 0)), wspec(0), wspec(nf),
                  cspec(width, 0), cspec(width, nf), cspec(1, 0), cspec(1, nf), bspec(0), bspec(nf)],
        out_specs=[pl.BlockSpec((tm, tn), lambda j, i: (i, j)), ospec, ospec],
        out_shape=[jax.ShapeDtypeStruct((m, f), BF16),
                   jax.ShapeDtypeStruct((bsz, width - 1, f), F32),
                   jax.ShapeDtypeStruct((bsz, width - 1, f), F32)],
        scratch_shapes=[pltpu.VMEM((d, tn), BF16), pltpu.VMEM((d, tn), BF16),
                        pltpu.VMEM((bb, tt + SUBLANES, tn), F32), pltpu.VMEM((bb, tt + SUBLANES, tn), F32)],
        compiler_params=pltpu.CompilerParams(
            dimension_semantics=("arbitrary", "arbitrary"), vmem_limit_bytes=VMEM_LIMIT),
        name=name,
    )(a, w, w, cw, cw, cb, cb, buf, buf)


def _norm_mod_kernel(x_ref, g_ref, sc_ref, sh_ref, o_ref):
    x = x_ref[...]
    y = x * lax.rsqrt(jnp.mean(x * x, axis=-1, keepdims=True) + RMS_EPS) * g_ref[...]
    y = y * (1.0 + sc_ref[...]) + sh_ref[...]
    o_ref[...] = y.reshape(o_ref.shape).astype(o_ref.dtype)


def _adaln_kernel(c_ref, w_ref, b_ref, o_ref):
    o_ref[...] = jnp.dot(_silu(c_ref[...]).astype(BF16), w_ref[...].astype(BF16),
                         preferred_element_type=F32) + b_ref[...]


def _adaln(c_all, w_ada, b_ada, *, tn):
    nb, d = c_all.shape
    depth, _, n = w_ada.shape
    tn = _pick_tile(n, tn)
    return pl.pallas_call(
        _adaln_kernel,
        grid=(depth, n // tn),
        in_specs=[pl.BlockSpec((nb, d), lambda l, j: (0, 0)),
                  pl.BlockSpec((None, d, tn), lambda l, j: (l, 0, j)),
                  pl.BlockSpec((None, 1, tn), lambda l, j: (l, 0, j))],
        out_specs=pl.BlockSpec((None, nb, tn), lambda l, j: (l, 0, j)),
        out_shape=jax.ShapeDtypeStruct((depth, nb, n), F32),
        compiler_params=pltpu.CompilerParams(
            dimension_semantics=("arbitrary", "arbitrary"), vmem_limit_bytes=VMEM_LIMIT),
        name="adaln",
    )(c_all, w_ada, b_ada)


def _norm_mod(x, g, layer, mod, sc_idx, sh_idx, boff, *, bb, tt, name):
    b, t, d = x.shape
    nt = t // tt
    assert boff % bb == 0
    return pl.pallas_call(
        _norm_mod_kernel,
        grid=(b // bb, nt),
        in_specs=[pl.BlockSpec((bb, tt, d), lambda i, j: (i, j, 0)),
                  pl.BlockSpec((None, 1, d), lambda i, j: (layer, 0, 0)),
                  pl.BlockSpec((None, bb, 1, d), lambda i, j: (layer, boff // bb + i, 0, sc_idx)),
                  pl.BlockSpec((None, bb, 1, d), lambda i, j: (layer, boff // bb + i, 0, sh_idx))],
        out_specs=pl.BlockSpec((bb * tt, d), lambda i, j: (i * nt + j, 0)),
        out_shape=jax.ShapeDtypeStruct((b * t, d), BF16),
        compiler_params=pltpu.CompilerParams(
            dimension_semantics=("arbitrary", "arbitrary"), vmem_limit_bytes=VMEM_LIMIT),
        name=name,
    )(x, g, mod, mod)


def _final_norm_kernel(x_ref, g_ref, o_ref):
    x = x_ref[...]
    o_ref[...] = x * lax.rsqrt(jnp.mean(x * x, axis=-1, keepdims=True) + RMS_EPS) * g_ref[...]


def _final_norm(x, g, *, bb, tt, name):
    b, t, d = x.shape
    return pl.pallas_call(
        _final_norm_kernel,
        grid=(b // bb, t // tt),
        in_specs=[pl.BlockSpec((bb, tt, d), lambda i, j: (i, j, 0)),
                  pl.BlockSpec((1, d), lambda i, j: (0, 0))],
        out_specs=pl.BlockSpec((bb, tt, d), lambda i, j: (i, j, 0)),
        out_shape=jax.ShapeDtypeStruct((b, t, d), F32),
        compiler_params=pltpu.CompilerParams(
            dimension_semantics=("arbitrary", "arbitrary"), vmem_limit_bytes=VMEM_LIMIT),
        name=name,
    )(x, g)


def _lru_kernel(x_ref, gate_ref, buf_ref, h0_ref, cw_ref, cb_ref, wa_ref, ba_ref, wi_ref, bi_ref,
                lam_ref, ga_ref, nbuf_ref, nh_ref, xx_ref, hc_ref, a_ref, b_ref, h_ref, *, bb, tt, width):
    c = x_ref.shape[-1]
    rows = bb * tt
    halo = width - 1
    base = SUBLANES - halo

    @pl.when(pl.program_id(1) == 0)
    def _init():
        xx_ref[:, base:SUBLANES, :] = buf_ref[...]
        hc_ref[...] = h0_ref[...]

    xx_ref[:, SUBLANES:SUBLANES + tt, :] = x_ref[...].reshape(bb, tt, c)
    conv = cb_ref[...] + cw_ref[0:1, :] * xx_ref[:, base:base + tt, :]
    for j in range(1, width):
        conv = conv + cw_ref[j:j + 1, :] * xx_ref[:, base + j:base + j + tt, :]
    last = xx_ref[:, tt + base:tt + SUBLANES, :]
    nbuf_ref[...] = last
    xx_ref[:, base:SUBLANES, :] = last

    xc = conv.reshape(rows, c)
    r = _sigmoid(_dot(xc, wa_ref[...]) + ba_ref[...])
    i = _sigmoid(_dot(xc, wi_ref[...]) + bi_ref[...])
    log_a = -LRU_C * r * _softplus(-lam_ref[...])
    a = jnp.exp(log_a)
    inp = jnp.sqrt(-jnp.tanh(log_a) * (a * a + 1.0)) * (i * xc)

    pos = lax.broadcasted_iota(jnp.int32, (rows, 1), 0) % SUBLANES
    for s in (1, 2, 4):
        a_sh = pltpu.roll(a, s, 0)
        b_sh = pltpu.roll(inp, s, 0)
        m = pos >= s
        inp = jnp.where(m, a * b_sh + inp, inp)
        a = jnp.where(m, a * a_sh, a)

    if tt == SUBLANES:
        h = (inp.reshape(bb, tt, c) + a.reshape(bb, tt, c) * hc_ref[...]).reshape(rows, c)
    else:
        a_ref[...] = a
        b_ref[...] = inp

        def group(g, carry):
            sl = pl.ds(pl.multiple_of(g * SUBLANES, SUBLANES), SUBLANES)
            hg = b_ref[sl, :] + a_ref[sl, :] * carry
            h_ref[sl, :] = hg
            return hg[SUBLANES - 1:SUBLANES, :]

        lax.fori_loop(0, rows // SUBLANES, group, hc_ref[0])
        h = h_ref[...]

    h3 = h.reshape(bb, tt, c)
    hc_ref[...] = h3[:, tt - 1:tt, :]
    nh_ref[...] = h3[:, tt - 1:tt, :]
    ga_ref[...] = (_gelu_tanh(gate_ref[...]) * h).astype(ga_ref.dtype)


def _lru_branch(p_lru, buf, h0, layer, P, *, t, bb, tt, name):
    b = p_lru.shape[0] // t
    c = h0.shape[-1]
    width = P["lru_conv_w"].shape[1]
    nt = t // tt
    assert t >= width - 1 and (tt == SUBLANES or bb == 1)
    lw = lambda shape: pl.BlockSpec((None,) + shape, lambda i, j: (layer,) + (0,) * len(shape))
    kern = functools.partial(_lru_kernel, bb=bb, tt=tt, width=width)
    return pl.pallas_call(
        kern,
        grid=(b // bb, nt),
        in_specs=[pl.BlockSpec((bb * tt, c), lambda i, j: (i * nt + j, 0)),
                  pl.BlockSpec((bb * tt, c), lambda i, j: (i * nt + j, 1)),
                  pl.BlockSpec((None, bb, width - 1, c), lambda i, j: (layer, i, 0, 0)),
                  pl.BlockSpec((None, bb, 1, c), lambda i, j: (layer, i, 0, 0)),
                  lw((width, c)), lw((1, c)), lw((c, c)), lw((1, c)), lw((c, c)), lw((1, c)), lw((1, c))],
        out_specs=[pl.BlockSpec((bb * tt, c), lambda i, j: (i * (t // tt) + j, 0)),
                   pl.BlockSpec((bb, width - 1, c), lambda i, j: (i, 0, 0)),
                   pl.BlockSpec((bb, 1, c), lambda i, j: (i, 0, 0))],
        out_shape=[jax.ShapeDtypeStruct((b * t, c), BF16),
                   jax.ShapeDtypeStruct((b, width - 1, c), F32),
                   jax.ShapeDtypeStruct((b, 1, c), F32)],
        scratch_shapes=[pltpu.VMEM((bb, tt + SUBLANES, c), F32),
                        pltpu.VMEM((bb, 1, c), F32),
                        pltpu.VMEM((bb * tt, c), F32),
                        pltpu.VMEM((bb * tt, c), F32),
                        pltpu.VMEM((bb * tt, c), F32)],
        compiler_params=pltpu.CompilerParams(
            dimension_semantics=("arbitrary", "arbitrary"), vmem_limit_bytes=VMEM_LIMIT),
        name=name,
    )(p_lru, p_lru, buf, h0, P["lru_conv_w"], P["lru_conv_b"], P["lru_wa_bd"], P["lru_ba"],
      P["lru_wi_bd"], P["lru_bi"], P["lru_lambda"])


def _rwkv_kernel(xa_ref, xb_ref, sbuf_ref, s0_ref, mu_ref, w0_ref, w2_ref, a0_ref, a2_ref, g2_ref, kkp_ref,
                 ka_ref, rk_ref, gng_ref, gnb_ref, seg_ref, segt_ref, ns_all_ref,
                 o_ref, nshift_ref, ns_ref,
                 sbd_ref, prev_ref, r_ref, kk_ref, km_ref, b_ref, v_ref, lw_ref, y_ref, tmp_ref,
                 *, bb, tt, chunk, dr):
    del ns_all_ref
    rows = bb * tt
    nrw = sbuf_ref.shape[-1]
    pairs = dr // LANES
    gb = RW_HEAD // chunk
    ti = pl.program_id(1)

    @pl.when(ti == 0)
    def _init():
        prev_ref[...] = sbuf_ref[...]
        tmp_ref[...] = jnp.zeros_like(tmp_ref)

        def pack(bi, carry):
            for j in range(pairs):
                sbd_ref[bi, j, 0:RW_HEAD, 0:RW_HEAD] = s0_ref[bi, 2 * j]
                sbd_ref[bi, j, 0:RW_HEAD, RW_HEAD:LANES] = jnp.zeros((RW_HEAD, RW_HEAD), F32)
                tmp_ref[:, 0:RW_HEAD] = s0_ref[bi, 2 * j + 1]
                sbd_ref[bi, j, RW_HEAD:ROWS, :] = pltpu.roll(tmp_ref[...], RW_HEAD, 1)
            return carry

        lax.fori_loop(0, bb, pack, 0)

    x2 = jnp.concatenate([xa_ref[...], xb_ref[:, 0:nrw - xa_ref.shape[-1]]], axis=-1)
    x3 = x2.reshape(bb, tt, nrw)
    rolled = pltpu.roll(x2, 1, 0).reshape(bb, tt, nrw)
    t_pos = lax.broadcasted_iota(jnp.int32, (bb, tt, 1), 1)
    prev3 = jnp.where(t_pos == 0, prev_ref[...], rolled)
    last = x3[:, tt - 1:tt, :]
    prev_ref[...] = last
    nshift_ref[...] = last
    xs = (x3 + (prev3 - x3) * mu_ref[...]).reshape(rows, nrw)

    r = xs[:, 0:dr]
    k = xs[:, dr:2 * dr]
    v = xs[:, 2 * dr:3 * dr]
    lora = xs[:, 3 * dr:]
    log_decay = -math.exp(-0.5) * _sigmoid(w0_ref[...] + _dot(jnp.tanh(lora), w2_ref[...]))
    a = _sigmoid(a0_ref[...] + _dot(lora, a2_ref[...]))
    gg = _dot(_sigmoid(lora), g2_ref[...])

    seg = seg_ref[...]
    segt = segt_ref[...]
    head_sum = lambda z, n=2: _dot_exact_rhs(z, seg, n)
    head_bcast = lambda z: _dot_exact_rhs(z, segt, 2)

    kk = k * kkp_ref[...]
    kk = kk * head_bcast(1.0 / jnp.maximum(jnp.sqrt(head_sum(kk * kk)), 1e-12))
    km = k * (1.0 + (a - 1.0) * ka_ref[...])
    r_ref[...] = r
    kk_ref[...] = kk
    km_ref[...] = km
    b_ref[...] = kk * a
    v_ref[...] = v
    lw_ref[...] = log_decay

    ri = lax.broadcasted_iota(jnp.int32, (ROWS, ROWS), 0)
    ci = lax.broadcasted_iota(jnp.int32, (ROWS, ROWS), 1)
    same = (ri // chunk) == (ci // chunk)
    m_strict = same & ((ci % chunk) < (ri % chunk))
    m_incl = same & ((ci % chunk) <= (ri % chunk))
    m_pair = (ri // RW_HEAD) == (ci // RW_HEAD)
    eye = jnp.where(ri == ci, 1.0, 0.0)
    lvl_masks = []
    s = 1
    while s < chunk:
        lvl_masks.append(((ri // (2 * s)) == (ci // (2 * s))) & ((ri // s) != (ci // s)))
        s *= 2
    r64 = lax.broadcasted_iota(jnp.int32, (RW_HEAD, RW_HEAD), 0)
    c64 = lax.broadcasted_iota(jnp.int32, (RW_HEAD, RW_HEAD), 1)
    tril = jnp.where(((r64 // chunk) == (c64 // chunk)) & (c64 <= r64), 1.0, 0.0).astype(BF16)
    lane_lo = lax.broadcasted_iota(jnp.int32, (1, 1, LANES), 2) < RW_HEAD

    def stack_par(z3):
        return jnp.concatenate([jnp.where(lane_lo, z3, 0.0), jnp.where(lane_lo, 0.0, z3)],
                               axis=1).reshape(ROWS, LANES)

    def stack_dup(z3):
        return jnp.concatenate([z3, z3], axis=1).reshape(ROWS, LANES)

    def row_chunk(rc, carry):
        sl = pl.ds(pl.multiple_of(rc * RW_HEAD, RW_HEAD), RW_HEAD)
        lw = lw_ref[sl, :]
        c_in = _dot_exact_lhs(tril, lw, 3)
        p_in = jnp.exp(c_in)
        p_inv = jnp.exp(-c_in)
        qt = kk_ref[sl, :] * jnp.exp(c_in - lw)
        rt = r_ref[sl, :] * p_in
        kt = km_ref[sl, :] * p_inv
        bt = b_ref[sl, :] * p_inv
        vv = v_ref[sl, :]
        p_end = p_in.reshape(gb, chunk, dr)[:, chunk - 1:chunk, :]
        pr = range(pairs)
        lanes = [slice(j * LANES, (j + 1) * LANES) for j in pr]
        split3 = lambda z: [z[:, ls].reshape(gb, chunk, LANES) for ls in lanes]
        q3, r3, k3, b3, v3 = split3(qt), split3(rt), split3(kt), split3(bt), split3(vv)
        lq = [stack_par(z) for z in q3]
        lr = [stack_par(z) for z in r3]
        vm = [stack_par(z) for z in v3]
        g = [_dot_nt(jnp.concatenate([lq[j], lr[j]], axis=0),
                     jnp.concatenate([stack_dup(b3[j]), stack_dup(k3[j])], axis=0)) for j in pr]
        l_b = [jnp.where(m_strict, g[j][0:ROWS, 0:ROWS], 0.0) for j in pr]
        m_k = [jnp.where(m_strict, g[j][0:ROWS, ROWS:], 0.0) for j in pr]
        n_kb = [jnp.concatenate([jnp.where(m_incl, g[j][ROWS:, ROWS:], 0.0),
                                 jnp.where(m_incl, -g[j][ROWS:, 0:ROWS], 0.0)], axis=1) for j in pr]
        mkv = [_dot(m_k[j], vm[j]) for j in pr]
        t_inv = [eye - jnp.where(lvl_masks[0], l_b[j], 0.0) for j in pr]
        for lm in lvl_masks[1:]:
            w = [_dot_hi(t_inv[j], jnp.where(lm, l_b[j], 0.0)) for j in pr]
            t_inv = [t_inv[j] - _dot_hi(w[j], t_inv[j]) for j in pr]
        bis = [0 if bb == 1 else (rc * RW_HEAD + s * chunk) // tt for s in range(gb)]
        st = [[sbd_ref[bis[s], j] for s in range(gb)] for j in pr]
        qa, ra = [], []
        for j in pr:
            lq3 = lq[j].reshape(gb, 2 * chunk, LANES)
            lr3 = lr[j].reshape(gb, 2 * chunk, LANES)
            qr = [_dot_nt(jnp.concatenate([lq3[s], lr3[s]], axis=0), st[j][s]) for s in range(gb)]
            qa.append(jnp.concatenate([z[0:2 * chunk] for z in qr], axis=0) if gb > 1 else qr[0][0:2 * chunk])
            ra.append(jnp.concatenate([z[2 * chunk:] for z in qr], axis=0) if gb > 1 else qr[0][2 * chunk:])
        x = [_dot_hi(t_inv[j], qa[j] + mkv[j]) for j in pr]
        ys = [ra[j] + _dot(n_kb[j], jnp.concatenate([vm[j], x[j]], axis=0)) for j in pr]
        for j in pr:
            ys3 = ys[j].reshape(gb, 2 * chunk, LANES)
            y_ref[sl, lanes[j]] = (ys3[:, 0:chunk] + ys3[:, chunk:]).reshape(RW_HEAD, LANES)
        for j in pr:
            u3 = x[j].reshape(gb, 2 * chunk, LANES)
            u3 = u3[:, 0:chunk] + u3[:, chunk:]
            for s in range(gb):
                ds = _dot_tn(jnp.concatenate([v3[j][s], -u3[s]], axis=0),
                             jnp.concatenate([k3[j][s], b3[j][s]], axis=0))
                sbd_ref[bis[s], j] = p_end[s][:, lanes[j]] * (st[j][s] + jnp.where(m_pair, ds, 0.0))
        return carry

    lax.fori_loop(0, rows // RW_HEAD, row_chunk, 0)

    y = y_ref[...]
    inv_n = 1.0 / RW_HEAD
    d = y - head_bcast(head_sum(y) * inv_n)
    rstd = lax.rsqrt(head_sum(d * d) * inv_n + GN_EPS)
    gn = d * head_bcast(rstd) * gng_ref[...] + gnb_ref[...]
    bonus = head_bcast(head_sum(r_ref[...] * km_ref[...] * rk_ref[...])) * v_ref[...]
    o_ref[...] = ((gn + bonus) * gg).astype(o_ref.dtype)

    @pl.when(ti == pl.num_programs(1) - 1)
    def _finish():
        def unpack(bi, carry):
            for j in range(pairs):
                ns_ref[bi, 2 * j] = sbd_ref[bi, j, 0:RW_HEAD, 0:RW_HEAD]
                ns_ref[bi, 2 * j + 1] = pltpu.roll(sbd_ref[bi, j, RW_HEAD:ROWS, :], RW_HEAD, 1)[:, 0:RW_HEAD]
            return carry

        lax.fori_loop(0, bb, unpack, 0)


def _rwkv_branch(p_all, rw_col, sbuf, s0, ns_all, layer, P, *, t, bb, tt, chunk, name):
    b = p_all.shape[0] // t
    nt = t // tt
    nrw = sbuf.shape[-1]
    heads = s0.shape[2]
    dr = heads * RW_HEAD
    nl = nrw - 3 * dr
    rows = bb * tt
    xw = rw_col
    assert xw % LANES == 0 and xw < nrw <= 2 * xw and rw_col + 2 * xw <= p_all.shape[-1]
    assert dr % LANES == 0 and RW_HEAD % chunk == 0 and tt % chunk == 0 and rows % RW_HEAD == 0
    assert tt == chunk or bb == 1
    lw = lambda shape: pl.BlockSpec((None,) + shape, lambda i, j: (layer,) + (0,) * len(shape))
    kern = functools.partial(_rwkv_kernel, bb=bb, tt=tt, chunk=chunk, dr=dr)
    big = lambda: pltpu.VMEM((rows, dr), F32)
    args = (p_all, p_all, sbuf, s0, P["rw_mu_p"], P["rw_w0"], P["rw_w2_p"], P["rw_a0"], P["rw_a2_p"], P["rw_g2_p"],
            P["rw_kk"], P["rw_ka"], P["rw_rk"], P["rw_gn_g"], P["rw_gn_b"], P["seg"], P["segt"], ns_all)
    return pl.pallas_call(
        kern,
        grid=(b // bb, t // tt),
        in_specs=[pl.BlockSpec((rows, xw), lambda i, j: (i * nt + j, 1)),
                  pl.BlockSpec((rows, xw), lambda i, j: (i * nt + j, 2)),
                  pl.BlockSpec((None, bb, 1, nrw), lambda i, j: (layer, i, 0, 0)),
                  pl.BlockSpec((None, bb, heads, RW_HEAD, RW_HEAD), lambda i, j: (layer, i, 0, 0, 0)),
                  lw((1, nrw)), lw((1, dr)), lw((nl, dr)), lw((1, dr)), lw((nl, dr)), lw((nl, dr)),
                  lw((1, dr)), lw((1, dr)), lw((1, dr)), lw((1, dr)), lw((1, dr)),
                  pl.BlockSpec((dr, LANES), lambda i, j: (0, 0)),
                  pl.BlockSpec((LANES, dr), lambda i, j: (0, 0)),
                  pl.BlockSpec(memory_space=pl.ANY)],
        out_specs=[pl.BlockSpec((rows, dr), lambda i, j: (i * (t // tt) + j, 0)),
                   pl.BlockSpec((bb, 1, nrw), lambda i, j: (i, 0, 0)),
                   pl.BlockSpec((None, bb, heads, RW_HEAD, RW_HEAD), lambda i, j: (layer, i, 0, 0, 0))],
        out_shape=[jax.ShapeDtypeStruct((b * t, dr), BF16),
                   jax.ShapeDtypeStruct((b, 1, nrw), F32),
                   jax.ShapeDtypeStruct(ns_all.shape, F32)],
        input_output_aliases={len(args) - 1: 2},
        scratch_shapes=[pltpu.VMEM((bb, dr // LANES, ROWS, LANES), F32),
                        pltpu.VMEM((bb, 1, nrw), F32),
                        big(), big(), big(), big(), big(), big(), big(),
                        pltpu.VMEM((RW_HEAD, LANES), F32)],
        compiler_params=pltpu.CompilerParams(
            dimension_semantics=("arbitrary", "arbitrary"), vmem_limit_bytes=VMEM_LIMIT),
        name=name,
    )(*args)


def _trunk(x, mod, boff, states, P, cfg, tag):
    bsz, t_len, d = x.shape
    m = bsz * t_len
    depth = mod.shape[0]
    lru_conv, lru_h, rw_shift, rw_s, ffn_conv = states
    dl = lru_h.shape[-1]
    nrw = P["rw_mu_p"].shape[-1]
    nrw0 = rw_shift.shape[-1]
    outs = ([], [], [], [])
    tm, tn = cfg["tm"], cfg["tn"]
    x = x.reshape(m, d)
    ns_all = jnp.zeros(rw_s.shape, F32)
    lru_h4 = lru_h[:, :, None, :]
    sbuf = jnp.pad(rw_shift, ((0, 0), (0, 0), (0, nrw - nrw0)))[:, :, None, :]
    norm = functools.partial(_norm_mod, bb=cfg["nbb"], tt=cfg["ntt"])
    for l in range(depth):
        h = norm(x.reshape(bsz, t_len, d), P["norm_mix"], l, mod, 1, 0, boff, name=f"norm_mix_{tag}")
        p_all, gates_col = _in_proj(h, P["w_in"], l, n_lru=2 * dl, n_rw=nrw, n_gates=2 * d, tm=tm, tn=tn,
                                    name=f"in_proj_{tag}")
        ga, n_lru_buf, n_lru_h = _lru_branch(p_all, lru_conv, lru_h4, l, P, t=t_len,
                                             bb=cfg["lbb"], tt=cfg["ltt"], name=f"lru_{tag}")
        o_rw, n_shift, ns_all = _rwkv_branch(p_all, 2 * dl, sbuf, rw_s, ns_all, l, P, t=t_len, bb=cfg["rbb"],
                                             tt=cfg["rtt"], chunk=cfg["chunk"], name=f"rwkv_{tag}")
        merged = _merge(ga, o_rw, P["w_pa"], P["w_pb"], p_all, gates_col, l, tm=tm, tn=cfg["tn_res"],
                        name=f"merge_{tag}")
        x = _matmul_resid(merged, P["w_o"], l, x, mod, 2, boff, t_len, tm=tm, tn=cfg["tn_res"], name=f"o_{tag}")

        h2 = norm(x.reshape(bsz, t_len, d), P["norm_ffn"], l, mod, 4, 3, boff, name=f"norm_ffn_{tag}")
        act, n_buf_g, n_buf_v = _ffn_up(h2, P["w_up"], P["ffn_conv_w"], P["ffn_conv_b"], ffn_conv, l,
                                        t_len=t_len, tm=tm, tn=cfg["tn_up"], name=f"ffn_up_{tag}")
        x = _matmul_resid(act, P["w_down"], l, x, mod, 5, boff, t_len, tm=cfg["tm_down"], tn=cfg["tn_res"],
                          name=f"ffn_down_{tag}")

        n_ffn_buf = jnp.concatenate([n_buf_g, n_buf_v], axis=-1)
        for lst, ns in zip(outs, (n_lru_buf, n_lru_h[:, 0, :], n_shift[:, 0, :nrw0], n_ffn_buf)):
            lst.append(ns)
    y = _final_norm(x.reshape(bsz, t_len, d), P["norm_final"], bb=cfg["nbb"], tt=cfg["ntt"], name=f"norm_final_{tag}")
    st = [jnp.stack(lst, axis=0) for lst in outs]
    return y, (st[0], st[1], st[2], ns_all, st[3])


def kernel(x_prompt, x_sample, c_prompt, c_sample, state_lru_conv, state_lru_h, state_rwkv_shift,
           state_rwkv_S, state_ffn_conv, w_ada, b_ada, norm_mix, norm_ffn, w_in, lru_conv_w,
           lru_conv_b, lru_wa, lru_ba, lru_wi, lru_bi, lru_lambda, w_pa, rw_mu, rw_w0, rw_w2, rw_a0,
           rw_a2, rw_g2, rw_kk, rw_ka, rw_rk, rw_gn_g, rw_gn_b, w_pb, w_o, w_up, ffn_conv_w,
           ffn_conv_b, w_down, norm_final):
    depth, d, _ = w_ada.shape
    bp, tp, _ = x_prompt.shape
    bs, ts, _ = x_sample.shape
    dl = lru_lambda.shape[-1]
    nblk, blk = lru_wa.shape[1], lru_wa.shape[2]
    heads = rw_rk.shape[1]
    dr = heads * RW_HEAD
    nrw0 = rw_mu.shape[-1]
    nl = _cdiv(nrw0 - 3 * dr, LANES) * LANES
    nrw = 3 * dr + nl
    lw_n, la_n = rw_w2.shape[1], rw_a2.shape[1]

    row = lambda p: p[:, None, :]
    eye = jnp.eye(nblk, dtype=F32)
    block_diag = lambda w: (eye[:, None, :, None] * w[:, :, :, None, :]).reshape(depth, dl, dl).astype(BF16)
    pad_rows = lambda w, off: jnp.pad(w, ((0, 0), (off, nl - off - w.shape[1]), (0, 0))).astype(BF16)
    head_of = jnp.arange(dr) // RW_HEAD
    seg = (head_of[:, None] == jnp.arange(LANES)[None, :]).astype(BF16)
    P = dict(
        norm_mix=row(norm_mix), norm_ffn=row(norm_ffn), norm_final=norm_final[None, :],
        w_in=w_in, lru_conv_w=lru_conv_w, lru_conv_b=row(lru_conv_b), lru_wa_bd=block_diag(lru_wa),
        lru_ba=row(lru_ba), lru_wi_bd=block_diag(lru_wi), lru_bi=row(lru_bi), lru_lambda=row(lru_lambda),
        w_pa=w_pa, w_pb=w_pb, w_o=w_o, w_up=w_up, w_down=w_down,
        rw_mu_p=row(jnp.pad(rw_mu, ((0, 0), (0, nrw - nrw0)))),
        rw_w0=row(rw_w0), rw_a0=row(rw_a0),
        rw_w2_p=pad_rows(rw_w2, 0), rw_a2_p=pad_rows(rw_a2, lw_n), rw_g2_p=pad_rows(rw_g2, lw_n + la_n),
        rw_kk=row(rw_kk), rw_ka=row(rw_ka), rw_rk=rw_rk.reshape(depth, 1, dr),
        rw_gn_g=row(rw_gn_g), rw_gn_b=row(rw_gn_b), seg=seg, segt=seg.T,
        ffn_conv_w=ffn_conv_w, ffn_conv_b=row(ffn_conv_b),
    )

    nb = bp + bs
    nb_pad = _cdiv(nb, SUBLANES) * SUBLANES
    c_all = jnp.concatenate([c_sample, c_prompt, jnp.zeros((nb_pad - nb, d), F32)], axis=0)
    mod = _adaln(c_all, w_ada, row(b_ada), tn=1024)[:, :, None, :]

    zeros = lambda *s: jnp.zeros((depth, bp) + s, F32)
    p_states = (zeros(lru_conv_w.shape[1] - 1, dl), zeros(dl), zeros(nrw0),
                zeros(heads, RW_HEAD, RW_HEAD), zeros(ffn_conv_w.shape[1] - 1, w_up.shape[-1]))
    s_states = (state_lru_conv, state_lru_h, state_rwkv_shift, state_rwkv_S, state_ffn_conv)

    tiles = dict(tm=1024, tn=1024, tn_res=512, tn_up=512, tm_down=512)
    cfg_p = dict(tiles, nbb=1, ntt=min(tp, 512), lbb=1, ltt=min(tp, 512),
                 rbb=1, rtt=min(tp, 256), chunk=min(tp, RW_HEAD))
    sb = min(bs, RW_HEAD // ts) if ts == SUBLANES else 1
    cfg_s = dict(tiles, nbb=min(bs, 64), ntt=ts, lbb=min(bs, 32), ltt=ts, rbb=sb, rtt=ts, chunk=ts)
    y_p, st_p = _trunk(x_prompt, mod, bs, p_states, P, cfg_p, "prompt")
    y_s, st_s = _trunk(x_sample, mod, 0, s_states, P, cfg_s, "sample")
    return (y_p, y_s) + st_p + st_s
```

```python
import functools
import math

import jax
import jax.numpy as jnp
from jax import lax
from jax.experimental import pallas as pl
from jax.experimental.pallas import tpu as pltpu

F32 = jnp.float32
BF16 = jnp.bfloat16

LANES = 128
SUBLANES = 8
RW_HEAD = 64
ROWS = 2 * RW_HEAD
LRU_C = 8.0
RMS_EPS = 1e-6
GN_EPS = 64e-5
VMEM_LIMIT = 56 * 1024 * 1024


def _cdiv(a, b):
    return -(-a // b)


def _dot(a, b):
    return jnp.dot(a.astype(BF16), b.astype(BF16), preferred_element_type=F32)


def _dot_nt(a, b):
    return lax.dot_general(a.astype(BF16), b.astype(BF16), (((1,), (1,)), ((), ())),
                           preferred_element_type=F32)


def _dot_tn(a, b):
    return lax.dot_general(a.astype(BF16), b.astype(BF16), (((0,), (0,)), ((), ())),
                           preferred_element_type=F32)


def _dot_hi(a, b):
    n = b.shape[1]
    a_hi, a_lo = _split(a, 2)
    b_hi, b_lo = _split(b, 2)
    lhs = jnp.concatenate([a_hi, a_lo], axis=1)
    rhs = jnp.concatenate([jnp.concatenate([b_hi, b_lo], axis=1),
                           jnp.concatenate([b_hi, jnp.zeros_like(b_lo)], axis=1)], axis=0)
    out = jnp.dot(lhs, rhs, preferred_element_type=F32)
    return out[:, :n] + out[:, n:]


def _split(x, n):
    parts = []
    for _ in range(n - 1):
        p = x.astype(BF16)
        parts.append(p)
        x = x - p.astype(F32)
    parts.append(x.astype(BF16))
    return parts


def _dot_exact_rhs(x, m, n=3):
    return sum(jnp.dot(p, m, preferred_element_type=F32) for p in _split(x, n))


def _dot_exact_lhs(m, x, n=3):
    return sum(jnp.dot(m, p, preferred_element_type=F32) for p in _split(x, n))


def _sigmoid(x):
    return 0.5 * jnp.tanh(0.5 * x) + 0.5


def _softplus(x):
    return jnp.maximum(x, 0.0) + jnp.log1p(jnp.exp(-jnp.abs(x)))


def _silu(x):
    return x * _sigmoid(x)


def _gelu_tanh(x):
    return 0.5 * x * (1.0 + jnp.tanh(math.sqrt(2.0 / math.pi) * (x + 0.044715 * (x * x * x))))


def _mm_resid_kernel(a_ref, w_ref, x_ref, gt_ref, o_ref, wbf_ref):
    @pl.when(pl.program_id(1) == 0)
    def _cast_weights():
        wbf_ref[...] = w_ref[...].astype(BF16)

    acc = jnp.dot(a_ref[...], wbf_ref[...], preferred_element_type=F32)
    nseq = gt_ref.shape[0]
    tm, tn = acc.shape
    out = x_ref[...].reshape(nseq, tm // nseq, tn) + gt_ref[...] * acc.reshape(nseq, tm // nseq, tn)
    o_ref[...] = out.reshape(tm, tn)


def _pick_tile(n, target):
    assert n % LANES == 0
    units = n // LANES
    best = max(u for u in range(1, units + 1) if units % u == 0 and u * LANES <= max(target, LANES))
    return best * LANES


def _gate_spec(layer, gate_col, boff, t_len, tm, tn, d):
    col = lambda j: gate_col * (d // tn) + j
    if tm <= t_len:
        assert t_len % tm == 0
        nt = t_len // tm
        return pl.BlockSpec((None, 1, 1, tn), lambda j, i: (layer, boff + i // nt, 0, col(j)))
    assert tm % t_len == 0 and boff % (tm // t_len) == 0
    nseq = tm // t_len
    return pl.BlockSpec((None, nseq, 1, tn), lambda j, i: (layer, boff // nseq + i, 0, col(j)))


def _matmul_resid(a, w, layer, x, mod, gate_col, boff, t_len, *, tm, tn, name):
    m, k = a.shape
    n = w.shape[-1]
    tm = min(tm, m)
    tn = _pick_tile(n, tn)
    assert m % tm == 0
    return pl.pallas_call(
        _mm_resid_kernel,
        grid=(n // tn, m // tm),
        in_specs=[pl.BlockSpec((tm, k), lambda j, i: (i, 0)),
                  pl.BlockSpec((None, k, tn), lambda j, i: (layer, 0, j)),
                  pl.BlockSpec((tm, tn), lambda j, i: (i, j)),
                  _gate_spec(layer, gate_col, boff, t_len, tm, tn, n)],
        out_specs=pl.BlockSpec((tm, tn), lambda j, i: (i, j)),
        out_shape=jax.ShapeDtypeStruct((m, n), F32),
        scratch_shapes=[pltpu.VMEM((k, tn), BF16)],
        compiler_params=pltpu.CompilerParams(
            dimension_semantics=("arbitrary", "arbitrary"), vmem_limit_bytes=VMEM_LIMIT),
        name=name,
    )(a, w, x, mod)


def _in_proj_kernel(a_ref, wa_ref, wb_ref, o_ref, wbf_ref, *, n_plain, shift):
    j = pl.program_id(0)
    first_row_tile = pl.program_id(1) == 0
    tn = wa_ref.shape[0]

    @pl.when(first_row_tile & (j < n_plain))
    def _cast_weights():
        wbf_ref[...] = wa_ref[...].astype(BF16)

    @pl.when(first_row_tile & (j >= n_plain))
    def _cast_shifted_weights():
        wbf_ref[0:tn - shift, :] = wa_ref[shift:tn, :].astype(BF16)
        if shift:
            wbf_ref[tn - shift:tn, :] = wb_ref[0:shift, :].astype(BF16)

    o_ref[...] = lax.dot_general(a_ref[...], wbf_ref[...], (((1,), (1,)), ((), ())),
                                 preferred_element_type=F32)


def _in_proj(a, w_in_t, layer, *, n_lru, n_rw, n_gates, tm, tn, name):
    m, k = a.shape
    n_in = w_in_t.shape[1]
    tm = min(tm, m)
    tn = _pick_tile(math.gcd(n_lru, n_gates), tn)
    n_rw_t = _cdiv(n_rw, tn) * tn
    gates_start = n_in - n_gates
    n_plain = (n_lru + n_rw_t) // tn
    base, shift = gates_start // tn, gates_start % tn
    bf16_rows = 2 * SUBLANES
    assert m % tm == 0 and n_lru + n_rw_t <= n_in and shift % bf16_rows == 0
    wbw = tn // 2 if (tn // 2) % LANES == 0 and shift <= tn // 2 else tn
    wa_idx = lambda j: jnp.where(j < n_plain, j, j - n_plain + base)
    wb_idx = lambda j: (jnp.maximum(j - n_plain, 0) + base + (1 if shift else 0)) * (tn // wbw)
    n_out = n_lru + n_rw_t + n_gates
    return pl.pallas_call(
        functools.partial(_in_proj_kernel, n_plain=n_plain, shift=shift),
        grid=(n_out // tn, m // tm),
        in_specs=[pl.BlockSpec((tm, k), lambda j, i: (i, 0)),
                  pl.BlockSpec((None, tn, k), lambda j, i: (layer, wa_idx(j), 0)),
                  pl.BlockSpec((None, wbw, k), lambda j, i: (layer, wb_idx(j), 0))],
        out_specs=pl.BlockSpec((tm, tn), lambda j, i: (i, j)),
        out_shape=jax.ShapeDtypeStruct((m, n_out), F32),
        scratch_shapes=[pltpu.VMEM((tn, k), BF16)],
        compiler_params=pltpu.CompilerParams(
            dimension_semantics=("arbitrary", "arbitrary"), vmem_limit_bytes=VMEM_LIMIT),
        name=name,
    )(a, w_in_t, w_in_t), n_lru + n_rw_t


def _merge_kernel(ga_ref, orw_ref, wpa_ref, wpb_ref, sa_ref, sb_ref, o_ref, wa_bf_ref, wb_bf_ref):
    @pl.when(pl.program_id(1) == 0)
    def _cast_weights():
        wa_bf_ref[...] = wpa_ref[...].astype(BF16)
        wb_bf_ref[...] = wpb_ref[...].astype(BF16)

    y_a = jnp.dot(ga_ref[...], wa_bf_ref[...], preferred_element_type=F32)
    y_b = jnp.dot(orw_ref[...], wb_bf_ref[...], preferred_element_type=F32)
    o_ref[...] = (_sigmoid(sa_ref[...]) * y_a + _sigmoid(sb_ref[...]) * y_b).astype(o_ref.dtype)


def _merge(ga, o_rw, w_pa, w_pb, p_all, gates_col, layer, *, tm, tn, name):
    m, ca = ga.shape
    cb = o_rw.shape[1]
    d = w_pa.shape[-1]
    tm = min(tm, m)
    tn = _pick_tile(math.gcd(d, gates_col), tn)
    off = gates_col // tn
    nd = d // tn
    assert m % tm == 0
    return pl.pallas_call(
        _merge_kernel,
        grid=(nd, m // tm),
        in_specs=[pl.BlockSpec((tm, ca), lambda j, i: (i, 0)),
                  pl.BlockSpec((tm, cb), lambda j, i: (i, 0)),
                  pl.BlockSpec((None, ca, tn), lambda j, i: (layer, 0, j)),
                  pl.BlockSpec((None, cb, tn), lambda j, i: (layer, 0, j)),
                  pl.BlockSpec((tm, tn), lambda j, i: (i, off + j)),
                  pl.BlockSpec((tm, tn), lambda j, i: (i, off + nd + j))],
        out_specs=pl.BlockSpec((tm, tn), lambda j, i: (i, j)),
        out_shape=jax.ShapeDtypeStruct((m, d), BF16),
        scratch_shapes=[pltpu.VMEM((ca, tn), BF16), pltpu.VMEM((cb, tn), BF16)],
        compiler_params=pltpu.CompilerParams(
            dimension_semantics=("arbitrary", "arbitrary"), vmem_limit_bytes=VMEM_LIMIT),
        name=name,
    )(ga, o_rw, w_pa, w_pb, p_all, p_all)


def _ffn_up_kernel(a_ref, wg_ref, wv_ref, cwg_ref, cwv_ref, cbg_ref, cbv_ref, bg_ref, bv_ref,
                   o_ref, nbg_ref, nbv_ref, wgb_ref, wvb_ref, xg_ref, xv_ref, *, bb, tt, nt, width):
    i = pl.program_id(1)
    tn = o_ref.shape[-1]
    base = SUBLANES - (width - 1)

    @pl.when(i == 0)
    def _cast_weights():
        wgb_ref[...] = wg_ref[...].astype(BF16)
        wvb_ref[...] = wv_ref[...].astype(BF16)

    @pl.when(i % nt == 0)
    def _sequence_start():
        xg_ref[:, base:SUBLANES, :] = bg_ref[...]
        xv_ref[:, base:SUBLANES, :] = bv_ref[...]

    a = a_ref[...]

    def half(wb_ref, x_ref, cw_ref, cb_ref, nb_ref):
        x_ref[:, SUBLANES:SUBLANES + tt, :] = jnp.dot(
            a, wb_ref[...], preferred_element_type=F32).reshape(bb, tt, tn)
        conv = cb_ref[...] + cw_ref[0:1, :] * x_ref[:, base:base + tt, :]
        for j in range(1, width):
            conv = conv + cw_ref[j:j + 1, :] * x_ref[:, base + j:base + j + tt, :]
        last = x_ref[:, tt + base:tt + SUBLANES, :]
        nb_ref[...] = last
        x_ref[:, base:SUBLANES, :] = last
        return conv

    cg = half(wgb_ref, xg_ref, cwg_ref, cbg_ref, nbg_ref)
    cv = half(wvb_ref, xv_ref, cwv_ref, cbv_ref, nbv_ref)
    o_ref[...] = (_silu(cg) * cv).reshape(bb * tt, tn).astype(o_ref.dtype)


def _ffn_up(a, w, cw, cb, buf, layer, *, t_len, tm, tn, name):
    m, d = a.shape
    f = w.shape[-1] // 2
    width = cw.shape[1]
    bsz = m // t_len
    tm = min(tm, m)
    tn = _pick_tile(f, tn)
    nf = f // tn
    if tm <= t_len:
        assert t_len % tm == 0
        bb, tt = 1, tm
    else:
        assert tm % t_len == 0 and t_len == SUBLANES
        bb, tt = tm // t_len, t_len
    nt = t_len // tt
    assert t_len >= width - 1
    wspec = lambda off: pl.BlockSpec((None, d, tn), lambda j, i: (layer, 0, j + off))
    cspec = lambda rows, off: pl.BlockSpec((None, rows, tn), lambda j, i: (layer, 0, j + off))
    bspec = lambda off: pl.BlockSpec((None, bb, width - 1, tn), lambda j, i: (layer, i // nt, 0, j + off))
    ospec = pl.BlockSpec((bb, width - 1, tn), lambda j, i: (i // nt, 0, j))
    return pl.pallas_call(
        functools.partial(_ffn_up_kernel, bb=bb, tt=tt, nt=nt, width=width),
        grid=(nf, m // tm),
        in_specs=[pl.BlockSpec((tm, d), lambda j, i: (i, 0)), wspec(0), wspec(nf),
                  cspec(width, 0), cspec(width, nf), cspec(1, 0), cspec(1, nf), bspec(0), bspec(nf)],
        out_specs=[pl.BlockSpec((tm, tn), lambda j, i: (i, j)), ospec, ospec],
        out_shape=[jax.ShapeDtypeStruct((m, f), BF16),
                   jax.ShapeDtypeStruct((bsz, width - 1, f), F32),
                   jax.ShapeDtypeStruct((bsz, width - 1, f), F32)],
        scratch_shapes=[pltpu.VMEM((d, tn), BF16), pltpu.VMEM((d, tn), BF16),
                        pltpu.VMEM((bb, tt + SUBLANES, tn), F32), pltpu.VMEM((bb, tt + SUBLANES, tn), F32)],
        compiler_params=pltpu.CompilerParams(
            dimension_semantics=("arbitrary", "arbitrary"), vmem_limit_bytes=VMEM_LIMIT),
        name=name,
    )(a, w, w, cw, cw, cb, cb, buf, buf)


def _norm_mod_kernel(x_ref, g_ref, sc_ref, sh_ref, o_ref):
    x = x_ref[...]
    y = x * lax.rsqrt(jnp.mean(x * x, axis=-1, keepdims=True) + RMS_EPS) * g_ref[...]
    y = y * (1.0 + sc_ref[...]) + sh_ref[...]
    o_ref[...] = y.reshape(o_ref.shape).astype(o_ref.dtype)


def _adaln_kernel(c_ref, w_ref, b_ref, o_ref):
    o_ref[...] = jnp.dot(_silu(c_ref[...]).astype(BF16), w_ref[...].astype(BF16),
                         preferred_element_type=F32) + b_ref[...]


def _adaln(c_all, w_ada, b_ada, *, tn):
    nb, d = c_all.shape
    depth, _, n = w_ada.shape
    tn = _pick_tile(n, tn)
    return pl.pallas_call(
        _adaln_kernel,
        grid=(depth, n // tn),
        in_specs=[pl.BlockSpec((nb, d), lambda l, j: (0, 0)),
                  pl.BlockSpec((None, d, tn), lambda l, j: (l, 0, j)),
                  pl.BlockSpec((None, 1, tn), lambda l, j: (l, 0, j))],
        out_specs=pl.BlockSpec((None, nb, tn), lambda l, j: (l, 0, j)),
        out_shape=jax.ShapeDtypeStruct((depth, nb, n), F32),
        compiler_params=pltpu.CompilerParams(
            dimension_semantics=("arbitrary", "arbitrary"), vmem_limit_bytes=VMEM_LIMIT),
        name="adaln",
    )(c_all, w_ada, b_ada)


def _norm_mod(x, g, layer, mod, sc_idx, sh_idx, boff, *, bb, tt, name):
    b, t, d = x.shape
    nt = t // tt
    assert boff % bb == 0
    return pl.pallas_call(
        _norm_mod_kernel,
        grid=(b // bb, nt),
        in_specs=[pl.BlockSpec((bb, tt, d), lambda i, j: (i, j, 0)),
                  pl.BlockSpec((None, 1, d), lambda i, j: (layer, 0, 0)),
                  pl.BlockSpec((None, bb, 1, d), lambda i, j: (layer, boff // bb + i, 0, sc_idx)),
                  pl.BlockSpec((None, bb, 1, d), lambda i, j: (layer, boff // bb + i, 0, sh_idx))],
        out_specs=pl.BlockSpec((bb * tt, d), lambda i, j: (i * nt + j, 0)),
        out_shape=jax.ShapeDtypeStruct((b * t, d), BF16),
        compiler_params=pltpu.CompilerParams(
            dimension_semantics=("arbitrary", "arbitrary"), vmem_limit_bytes=VMEM_LIMIT),
        name=name,
    )(x, g, mod, mod)


def _final_norm_kernel(x_ref, g_ref, o_ref):
    x = x_ref[...]
    o_ref[...] = x * lax.rsqrt(jnp.mean(x * x, axis=-1, keepdims=True) + RMS_EPS) * g_ref[...]


def _final_norm(x, g, *, bb, tt, name):
    b, t, d = x.shape
    return pl.pallas_call(
        _final_norm_kernel,
        grid=(b // bb, t // tt),
        in_specs=[pl.BlockSpec((bb, tt, d), lambda i, j: (i, j, 0)),
                  pl.BlockSpec((1, d), lambda i, j: (0, 0))],
        out_specs=pl.BlockSpec((bb, tt, d), lambda i, j: (i, j, 0)),
        out_shape=jax.ShapeDtypeStruct((b, t, d), F32),
        compiler_params=pltpu.CompilerParams(
            dimension_semantics=("arbitrary", "arbitrary"), vmem_limit_bytes=VMEM_LIMIT),
        name=name,
    )(x, g)


def _lru_kernel(x_ref, gate_ref, buf_ref, h0_ref, cw_ref, cb_ref, wa_ref, ba_ref, wi_ref, bi_ref,
                lam_ref, ga_ref, nbuf_ref, nh_ref, xx_ref, hc_ref, a_ref, b_ref, h_ref, *, bb, tt, width):
    c = x_ref.shape[-1]
    rows = bb * tt
    halo = width - 1
    base = SUBLANES - halo

    @pl.when(pl.program_id(1) == 0)
    def _init():
        xx_ref[:, base:SUBLANES, :] = buf_ref[...]
        hc_ref[...] = h0_ref[...]

    xx_ref[:, SUBLANES:SUBLANES + tt, :] = x_ref[...].reshape(bb, tt, c)
    conv = cb_ref[...] + cw_ref[0:1, :] * xx_ref[:, base:base + tt, :]
    for j in range(1, width):
        conv = conv + cw_ref[j:j + 1, :] * xx_ref[:, base + j:base + j + tt, :]
    last = xx_ref[:, tt + base:tt + SUBLANES, :]
    nbuf_ref[...] = last
    xx_ref[:, base:SUBLANES, :] = last

    xc = conv.reshape(rows, c)
    xcb = xc.astype(BF16)

    def gate(w_ref, b_ref):
        pre = [jnp.dot(xcb[:, g * LANES:(g + 1) * LANES], w_ref[g], preferred_element_type=F32)
               for g in range(c // LANES)]
        return _sigmoid(jnp.concatenate(pre, axis=1) + b_ref[...])

    r = gate(wa_ref, ba_ref)
    i = gate(wi_ref, bi_ref)
    log_a = -LRU_C * r * _softplus(-lam_ref[...])
    a = jnp.exp(log_a)
    inp = jnp.sqrt(-jnp.tanh(log_a) * (a * a + 1.0)) * (i * xc)

    pos = lax.broadcasted_iota(jnp.int32, (rows, 1), 0) % SUBLANES
    for s in (1, 2, 4):
        a_sh = pltpu.roll(a, s, 0)
        b_sh = pltpu.roll(inp, s, 0)
        m = pos >= s
        inp = jnp.where(m, a * b_sh + inp, inp)
        a = jnp.where(m, a * a_sh, a)

    if tt == SUBLANES:
        h = (inp.reshape(bb, tt, c) + a.reshape(bb, tt, c) * hc_ref[...]).reshape(rows, c)
    else:
        a_ref[...] = a
        b_ref[...] = inp

        def group(g, carry):
            sl = pl.ds(pl.multiple_of(g * SUBLANES, SUBLANES), SUBLANES)
            hg = b_ref[sl, :] + a_ref[sl, :] * carry
            h_ref[sl, :] = hg
            return hg[SUBLANES - 1:SUBLANES, :]

        lax.fori_loop(0, rows // SUBLANES, group, hc_ref[0])
        h = h_ref[...]

    h3 = h.reshape(bb, tt, c)
    hc_ref[...] = h3[:, tt - 1:tt, :]
    nh_ref[...] = h3[:, tt - 1:tt, :]
    ga_ref[...] = (_gelu_tanh(gate_ref[...]) * h).astype(ga_ref.dtype)


def _lru_branch(p_lru, buf, h0, layer, P, *, t, bb, tt, name):
    b = p_lru.shape[0] // t
    c = h0.shape[-1]
    width = P["lru_conv_w"].shape[1]
    nt = t // tt
    assert t >= width - 1 and (tt == SUBLANES or bb == 1)
    lw = lambda shape: pl.BlockSpec((None,) + shape, lambda i, j: (layer,) + (0,) * len(shape))
    kern = functools.partial(_lru_kernel, bb=bb, tt=tt, width=width)
    return pl.pallas_call(
        kern,
        grid=(b // bb, nt),
        in_specs=[pl.BlockSpec((bb * tt, c), lambda i, j: (i * nt + j, 0)),
                  pl.BlockSpec((bb * tt, c), lambda i, j: (i * nt + j, 1)),
                  pl.BlockSpec((None, bb, width - 1, c), lambda i, j: (layer, i, 0, 0)),
                  pl.BlockSpec((None, bb, 1, c), lambda i, j: (layer, i, 0, 0)),
                  lw((width, c)), lw((1, c)), lw((c // LANES, LANES, LANES)), lw((1, c)),
                  lw((c // LANES, LANES, LANES)), lw((1, c)), lw((1, c))],
        out_specs=[pl.BlockSpec((bb * tt, c), lambda i, j: (i * (t // tt) + j, 0)),
                   pl.BlockSpec((bb, width - 1, c), lambda i, j: (i, 0, 0)),
                   pl.BlockSpec((bb, 1, c), lambda i, j: (i, 0, 0))],
        out_shape=[jax.ShapeDtypeStruct((b * t, c), BF16),
                   jax.ShapeDtypeStruct((b, width - 1, c), F32),
                   jax.ShapeDtypeStruct((b, 1, c), F32)],
        scratch_shapes=[pltpu.VMEM((bb, tt + SUBLANES, c), F32),
                        pltpu.VMEM((bb, 1, c), F32),
                        pltpu.VMEM((bb * tt, c), F32),
                        pltpu.VMEM((bb * tt, c), F32),
                        pltpu.VMEM((bb * tt, c), F32)],
        compiler_params=pltpu.CompilerParams(
            dimension_semantics=("arbitrary", "arbitrary"), vmem_limit_bytes=VMEM_LIMIT),
        name=name,
    )(p_lru, p_lru, buf, h0, P["lru_conv_w"], P["lru_conv_b"], P["lru_wa_bd"], P["lru_ba"],
      P["lru_wi_bd"], P["lru_bi"], P["lru_lambda"])


def _rwkv_kernel(xa_ref, xb_ref, sbuf_ref, s0_ref, mu_ref, w0_ref, w2_ref, a0_ref, a2_ref, g2_ref, kkp_ref,
                 ka_ref, rk_ref, gng_ref, gnb_ref, seg_ref, segt_ref, ns_all_ref,
                 o_ref, nshift_ref, ns_ref,
                 sbd_ref, prev_ref, r_ref, kk_ref, km_ref, b_ref, v_ref, lw_ref, y_ref, tmp_ref,
                 *, bb, tt, chunk, dr):
    del ns_all_ref
    rows = bb * tt
    nrw = sbuf_ref.shape[-1]
    pairs = dr // LANES
    gb = RW_HEAD // chunk
    ti = pl.program_id(1)

    @pl.when(ti == 0)
    def _init():
        prev_ref[...] = sbuf_ref[...]
        tmp_ref[...] = jnp.zeros_like(tmp_ref)

        def pack(bi, carry):
            for j in range(pairs):
                sbd_ref[bi, j, 0:RW_HEAD, 0:RW_HEAD] = s0_ref[bi, 2 * j]
                sbd_ref[bi, j, 0:RW_HEAD, RW_HEAD:LANES] = jnp.zeros((RW_HEAD, RW_HEAD), F32)
                tmp_ref[:, 0:RW_HEAD] = s0_ref[bi, 2 * j + 1]
                sbd_ref[bi, j, RW_HEAD:ROWS, :] = pltpu.roll(tmp_ref[...], RW_HEAD, 1)
            return carry

        lax.fori_loop(0, bb, pack, 0)

    x2 = jnp.concatenate([xa_ref[...], xb_ref[:, 0:nrw - xa_ref.shape[-1]]], axis=-1)
    x3 = x2.reshape(bb, tt, nrw)
    rolled = pltpu.roll(x2, 1, 0).reshape(bb, tt, nrw)
    t_pos = lax.broadcasted_iota(jnp.int32, (bb, tt, 1), 1)
    prev3 = jnp.where(t_pos == 0, prev_ref[...], rolled)
    last = x3[:, tt - 1:tt, :]
    prev_ref[...] = last
    nshift_ref[...] = last
    xs = (x3 + (prev3 - x3) * mu_ref[...]).reshape(rows, nrw)

    r = xs[:, 0:dr]
    k = xs[:, dr:2 * dr]
    v = xs[:, 2 * dr:3 * dr]
    lora = xs[:, 3 * dr:]
    log_decay = -math.exp(-0.5) * _sigmoid(w0_ref[...] + _dot(jnp.tanh(lora), w2_ref[...]))
    a = _sigmoid(a0_ref[...] + _dot(lora, a2_ref[...]))
    gg = _dot(_sigmoid(lora), g2_ref[...])

    seg = seg_ref[...]
    segt = segt_ref[...]
    head_sum = lambda z, n=2: _dot_exact_rhs(z, seg, n)
    head_bcast = lambda z: _dot_exact_rhs(z, segt, 2)

    kk = k * kkp_ref[...]
    kk = kk * head_bcast(1.0 / jnp.maximum(jnp.sqrt(head_sum(kk * kk)), 1e-12))
    km = k * (1.0 + (a - 1.0) * ka_ref[...])
    r_ref[...] = r
    kk_ref[...] = kk
    km_ref[...] = km
    b_ref[...] = kk * a
    v_ref[...] = v
    lw_ref[...] = log_decay

    ri = lax.broadcasted_iota(jnp.int32, (ROWS, ROWS), 0)
    ci = lax.broadcasted_iota(jnp.int32, (ROWS, ROWS), 1)
    same = (ri // chunk) == (ci // chunk)
    m_strict = same & ((ci % chunk) < (ri % chunk))
    m_incl = same & ((ci % chunk) <= (ri % chunk))
    m_pair = (ri // RW_HEAD) == (ci // RW_HEAD)
    eye = jnp.where(ri == ci, 1.0, 0.0)
    lvl_masks = []
    s = 1
    while s < chunk:
        lvl_masks.append(((ri // (2 * s)) == (ci // (2 * s))) & ((ri // s) != (ci // s)))
        s *= 2
    r64 = lax.broadcasted_iota(jnp.int32, (RW_HEAD, RW_HEAD), 0)
    c64 = lax.broadcasted_iota(jnp.int32, (RW_HEAD, RW_HEAD), 1)
    tril = jnp.where(((r64 // chunk) == (c64 // chunk)) & (c64 <= r64), 1.0, 0.0).astype(BF16)
    lane_lo = lax.broadcasted_iota(jnp.int32, (1, 1, LANES), 2) < RW_HEAD

    def stack_par(z3):
        return jnp.concatenate([jnp.where(lane_lo, z3, 0.0), jnp.where(lane_lo, 0.0, z3)],
                               axis=1).reshape(ROWS, LANES)

    def stack_dup(z3):
        return jnp.concatenate([z3, z3], axis=1).reshape(ROWS, LANES)

    def row_chunk(rc, carry):
        sl = pl.ds(pl.multiple_of(rc * RW_HEAD, RW_HEAD), RW_HEAD)
        lw = lw_ref[sl, :]
        c_in = _dot_exact_lhs(tril, lw, 3)
        p_in = jnp.exp(c_in)
        p_inv = jnp.exp(-c_in)
        qt = kk_ref[sl, :] * jnp.exp(c_in - lw)
        rt = r_ref[sl, :] * p_in
        kt = km_ref[sl, :] * p_inv
        bt = b_ref[sl, :] * p_inv
        vv = v_ref[sl, :]
        p_end = p_in.reshape(gb, chunk, dr)[:, chunk - 1:chunk, :]
        pr = range(pairs)
        lanes = [slice(j * LANES, (j + 1) * LANES) for j in pr]
        split3 = lambda z: [z[:, ls].reshape(gb, chunk, LANES) for ls in lanes]
        q3, r3, k3, b3, v3 = split3(qt), split3(rt), split3(kt), split3(bt), split3(vv)
        lq = [stack_par(z) for z in q3]
        lr = [stack_par(z) for z in r3]
        vm = [stack_par(z) for z in v3]
        g = [_dot_nt(jnp.concatenate([lq[j], lr[j]], axis=0),
                     jnp.concatenate([stack_dup(b3[j]), stack_dup(k3[j])], axis=0)) for j in pr]
        l_b = [jnp.where(m_strict, g[j][0:ROWS, 0:ROWS], 0.0) for j in pr]
        m_k = [jnp.where(m_strict, g[j][0:ROWS, ROWS:], 0.0) for j in pr]
        n_kb = [jnp.concatenate([jnp.where(m_incl, g[j][ROWS:, ROWS:], 0.0),
                                 jnp.where(m_incl, -g[j][ROWS:, 0:ROWS], 0.0)], axis=1) for j in pr]
        mkv = [_dot(m_k[j], vm[j]) for j in pr]
        t_inv = [eye - jnp.where(lvl_masks[0], l_b[j], 0.0) for j in pr]
        for lm in lvl_masks[1:]:
            w = [_dot_hi(t_inv[j], jnp.where(lm, l_b[j], 0.0)) for j in pr]
            t_inv = [t_inv[j] - _dot_hi(w[j], t_inv[j]) for j in pr]
        bis = [0 if bb == 1 else (rc * RW_HEAD + s * chunk) // tt for s in range(gb)]
        st = [[sbd_ref[bis[s], j] for s in range(gb)] for j in pr]
        qa, ra = [], []
        for j in pr:
            lq3 = lq[j].reshape(gb, 2 * chunk, LANES)
            lr3 = lr[j].reshape(gb, 2 * chunk, LANES)
            qr = [_dot_nt(jnp.concatenate([lq3[s], lr3[s]], axis=0), st[j][s]) for s in range(gb)]
            qa.append(jnp.concatenate([z[0:2 * chunk] for z in qr], axis=0) if gb > 1 else qr[0][0:2 * chunk])
            ra.append(jnp.concatenate([z[2 * chunk:] for z in qr], axis=0) if gb > 1 else qr[0][2 * chunk:])
        x = [_dot_hi(t_inv[j], qa[j] + mkv[j]) for j in pr]
        ys = [ra[j] + _dot(n_kb[j], jnp.concatenate([vm[j], x[j]], axis=0)) for j in pr]
        for j in pr:
            ys3 = ys[j].reshape(gb, 2 * chunk, LANES)
            y_ref[sl, lanes[j]] = (ys3[:, 0:chunk] + ys3[:, chunk:]).reshape(RW_HEAD, LANES)
        for j in pr:
            u3 = x[j].reshape(gb, 2 * chunk, LANES)
            u3 = u3[:, 0:chunk] + u3[:, chunk:]
            for s in range(gb):
                ds = _dot_tn(jnp.concatenate([v3[j][s], -u3[s]], axis=0),
                             jnp.concatenate([k3[j][s], b3[j][s]], axis=0))
                sbd_ref[bis[s], j] = p_end[s][:, lanes[j]] * (st[j][s] + jnp.where(m_pair, ds, 0.0))
        return carry

    lax.fori_loop(0, rows // RW_HEAD, row_chunk, 0)

    y = y_ref[...]
    inv_n = 1.0 / RW_HEAD
    d = y - head_bcast(head_sum(y) * inv_n)
    rstd = lax.rsqrt(head_sum(d * d) * inv_n + GN_EPS)
    gn = d * head_bcast(rstd) * gng_ref[...] + gnb_ref[...]
    bonus = head_bcast(head_sum(r_ref[...] * km_ref[...] * rk_ref[...])) * v_ref[...]
    o_ref[...] = ((gn + bonus) * gg).astype(o_ref.dtype)

    @pl.when(ti == pl.num_programs(1) - 1)
    def _finish():
        def unpack(bi, carry):
            for j in range(pairs):
                ns_ref[bi, 2 * j] = sbd_ref[bi, j, 0:RW_HEAD, 0:RW_HEAD]
                ns_ref[bi, 2 * j + 1] = pltpu.roll(sbd_ref[bi, j, RW_HEAD:ROWS, :], RW_HEAD, 1)[:, 0:RW_HEAD]
            return carry

        lax.fori_loop(0, bb, unpack, 0)


def _rwkv_branch(p_all, rw_col, sbuf, s0, ns_all, layer, P, *, t, bb, tt, chunk, name):
    b = p_all.shape[0] // t
    nt = t // tt
    nrw = sbuf.shape[-1]
    heads = s0.shape[2]
    dr = heads * RW_HEAD
    nl = nrw - 3 * dr
    rows = bb * tt
    xw = rw_col
    assert xw % LANES == 0 and xw < nrw <= 2 * xw and rw_col + 2 * xw <= p_all.shape[-1]
    assert dr % LANES == 0 and RW_HEAD % chunk == 0 and tt % chunk == 0 and rows % RW_HEAD == 0
    assert tt == chunk or bb == 1
    lw = lambda shape: pl.BlockSpec((None,) + shape, lambda i, j: (layer,) + (0,) * len(shape))
    kern = functools.partial(_rwkv_kernel, bb=bb, tt=tt, chunk=chunk, dr=dr)
    big = lambda: pltpu.VMEM((rows, dr), F32)
    args = (p_all, p_all, sbuf, s0, P["rw_mu_p"], P["rw_w0"], P["rw_w2_p"], P["rw_a0"], P["rw_a2_p"], P["rw_g2_p"],
            P["rw_kk"], P["rw_ka"], P["rw_rk"], P["rw_gn_g"], P["rw_gn_b"], P["seg"], P["segt"], ns_all)
    return pl.pallas_call(
        kern,
        grid=(b // bb, t // tt),
        in_specs=[pl.BlockSpec((rows, xw), lambda i, j: (i * nt + j, 1)),
                  pl.BlockSpec((rows, xw), lambda i, j: (i * nt + j, 2)),
                  pl.BlockSpec((None, bb, 1, nrw), lambda i, j: (layer, i, 0, 0)),
                  pl.BlockSpec((None, bb, heads, RW_HEAD, RW_HEAD), lambda i, j: (layer, i, 0, 0, 0)),
                  lw((1, nrw)), lw((1, dr)), lw((nl, dr)), lw((1, dr)), lw((nl, dr)), lw((nl, dr)),
                  lw((1, dr)), lw((1, dr)), lw((1, dr)), lw((1, dr)), lw((1, dr)),
                  pl.BlockSpec((dr, LANES), lambda i, j: (0, 0)),
                  pl.BlockSpec((LANES, dr), lambda i, j: (0, 0)),
                  pl.BlockSpec(memory_space=pl.ANY)],
        out_specs=[pl.BlockSpec((rows, dr), lambda i, j: (i * (t // tt) + j, 0)),
                   pl.BlockSpec((bb, 1, nrw), lambda i, j: (i, 0, 0)),
                   pl.BlockSpec((None, bb, heads, RW_HEAD, RW_HEAD), lambda i, j: (layer, i, 0, 0, 0))],
        out_shape=[jax.ShapeDtypeStruct((b * t, dr), BF16),
                   jax.ShapeDtypeStruct((b, 1, nrw), F32),
                   jax.ShapeDtypeStruct(ns_all.shape, F32)],
        input_output_aliases={len(args) - 1: 2},
        scratch_shapes=[pltpu.VMEM((bb, dr // LANES, ROWS, LANES), F32),
                        pltpu.VMEM((bb, 1, nrw), F32),
                        big(), big(), big(), big(), big(), big(), big(),
                        pltpu.VMEM((RW_HEAD, LANES), F32)],
        compiler_params=pltpu.CompilerParams(
            dimension_semantics=("arbitrary", "arbitrary"), vmem_limit_bytes=VMEM_LIMIT),
        name=name,
    )(*args)


def _trunk(x, mod, boff, states, P, cfg, tag):
    bsz, t_len, d = x.shape
    m = bsz * t_len
    depth = mod.shape[0]
    lru_conv, lru_h, rw_shift, rw_s, ffn_conv = states
    dl = lru_h.shape[-1]
    nrw = P["rw_mu_p"].shape[-1]
    nrw0 = rw_shift.shape[-1]
    outs = ([], [], [], [])
    tm, tn = cfg["tm"], cfg["tn"]
    x = x.reshape(m, d)
    ns_all = jnp.zeros(rw_s.shape, F32)
    lru_h4 = lru_h[:, :, None, :]
    sbuf = jnp.pad(rw_shift, ((0, 0), (0, 0), (0, nrw - nrw0)))[:, :, None, :]
    norm = functools.partial(_norm_mod, bb=cfg["nbb"], tt=cfg["ntt"])
    for l in range(depth):
        h = norm(x.reshape(bsz, t_len, d), P["norm_mix"], l, mod, 1, 0, boff, name=f"norm_mix_{tag}")
        p_all, gates_col = _in_proj(h, P["w_in_t"], l, n_lru=2 * dl, n_rw=nrw, n_gates=2 * d, tm=tm, tn=tn,
                                    name=f"in_proj_{tag}")
        ga, n_lru_buf, n_lru_h = _lru_branch(p_all, lru_conv, lru_h4, l, P, t=t_len,
                                             bb=cfg["lbb"], tt=cfg["ltt"], name=f"lru_{tag}")
        o_rw, n_shift, ns_all = _rwkv_branch(p_all, 2 * dl, sbuf, rw_s, ns_all, l, P, t=t_len, bb=cfg["rbb"],
                                             tt=cfg["rtt"], chunk=cfg["chunk"], name=f"rwkv_{tag}")
        merged = _merge(ga, o_rw, P["w_pa"], P["w_pb"], p_all, gates_col, l, tm=tm, tn=cfg["tn_res"],
                        name=f"merge_{tag}")
        x = _matmul_resid(merged, P["w_o"], l, x, mod, 2, boff, t_len, tm=tm, tn=tn, name=f"o_{tag}")

        h2 = norm(x.reshape(bsz, t_len, d), P["norm_ffn"], l, mod, 4, 3, boff, name=f"norm_ffn_{tag}")
        act, n_buf_g, n_buf_v = _ffn_up(h2, P["w_up"], P["ffn_conv_w"], P["ffn_conv_b"], ffn_conv, l,
                                        t_len=t_len, tm=tm, tn=cfg["tn_up"], name=f"ffn_up_{tag}")
        x = _matmul_resid(act, P["w_down"], l, x, mod, 5, boff, t_len, tm=cfg["tm_down"], tn=cfg["tn_res"],
                          name=f"ffn_down_{tag}")

        n_ffn_buf = jnp.concatenate([n_buf_g, n_buf_v], axis=-1)
        for lst, ns in zip(outs, (n_lru_buf, n_lru_h[:, 0, :], n_shift[:, 0, :nrw0], n_ffn_buf)):
            lst.append(ns)
    y = _final_norm(x.reshape(bsz, t_len, d), P["norm_final"], bb=cfg["nbb"], tt=cfg["ntt"], name=f"norm_final_{tag}")
    st = [jnp.stack(lst, axis=0) for lst in outs]
    return y, (st[0], st[1], st[2], ns_all, st[3])


def kernel(x_prompt, x_sample, c_prompt, c_sample, state_lru_conv, state_lru_h, state_rwkv_shift,
           state_rwkv_S, state_ffn_conv, w_ada, b_ada, norm_mix, norm_ffn, w_in, lru_conv_w,
           lru_conv_b, lru_wa, lru_ba, lru_wi, lru_bi, lru_lambda, w_pa, rw_mu, rw_w0, rw_w2, rw_a0,
           rw_a2, rw_g2, rw_kk, rw_ka, rw_rk, rw_gn_g, rw_gn_b, w_pb, w_o, w_up, ffn_conv_w,
           ffn_conv_b, w_down, norm_final):
    depth, d, _ = w_ada.shape
    bp, tp, _ = x_prompt.shape
    bs, ts, _ = x_sample.shape
    dl = lru_lambda.shape[-1]
    nblk, blk = lru_wa.shape[1], lru_wa.shape[2]
    heads = rw_rk.shape[1]
    dr = heads * RW_HEAD
    nrw0 = rw_mu.shape[-1]
    nl = _cdiv(nrw0 - 3 * dr, LANES) * LANES
    nrw = 3 * dr + nl
    lw_n, la_n = rw_w2.shape[1], rw_a2.shape[1]

    row = lambda p: p[:, None, :]
    per = LANES // blk
    eye = jnp.eye(per, dtype=F32)
    block_diag = lambda w: (eye[:, None, :, None] * w.reshape(depth, nblk // per, per, blk, 1, blk)
                            ).reshape(depth, nblk // per, LANES, LANES).astype(BF16)
    pad_rows = lambda w, off: jnp.pad(w, ((0, 0), (off, nl - off - w.shape[1]), (0, 0))).astype(BF16)
    head_of = jnp.arange(dr) // RW_HEAD
    seg = (head_of[:, None] == jnp.arange(LANES)[None, :]).astype(BF16)
    P = dict(
        norm_mix=row(norm_mix), norm_ffn=row(norm_ffn), norm_final=norm_final[None, :],
        w_in_t=jnp.swapaxes(w_in, 1, 2), lru_conv_w=lru_conv_w, lru_conv_b=row(lru_conv_b), lru_wa_bd=block_diag(lru_wa),
        lru_ba=row(lru_ba), lru_wi_bd=block_diag(lru_wi), lru_bi=row(lru_bi), lru_lambda=row(lru_lambda),
        w_pa=w_pa, w_pb=w_pb, w_o=w_o, w_up=w_up, w_down=w_down,
        rw_mu_p=row(jnp.pad(rw_mu, ((0, 0), (0, nrw - nrw0)))),
        rw_w0=row(rw_w0), rw_a0=row(rw_a0),
        rw_w2_p=pad_rows(rw_w2, 0), rw_a2_p=pad_rows(rw_a2, lw_n), rw_g2_p=pad_rows(rw_g2, lw_n + la_n),
        rw_kk=row(rw_kk), rw_ka=row(rw_ka), rw_rk=rw_rk.reshape(depth, 1, dr),
        rw_gn_g=row(rw_gn_g), rw_gn_b=row(rw_gn_b), seg=seg, segt=seg.T,
        ffn_conv_w=ffn_conv_w, ffn_conv_b=row(ffn_conv_b),
    )

    nb = bp + bs
    nb_pad = _cdiv(nb, SUBLANES) * SUBLANES
    c_all = jnp.concatenate([c_sample, c_prompt, jnp.zeros((nb_pad - nb, d), F32)], axis=0)
    mod = _adaln(c_all, w_ada, row(b_ada), tn=1024)[:, :, None, :]

    zeros = lambda *s: jnp.zeros((depth, bp) + s, F32)
    p_states = (zeros(lru_conv_w.shape[1] - 1, dl), zeros(dl), zeros(nrw0),
                zeros(heads, RW_HEAD, RW_HEAD), zeros(ffn_conv_w.shape[1] - 1, w_up.shape[-1]))
    s_states = (state_lru_conv, state_lru_h, state_rwkv_shift, state_rwkv_S, state_ffn_conv)

    tiles = dict(tm=1024, tn=1024, tn_res=512, tn_up=512, tm_down=512)
    cfg_p = dict(tiles, nbb=1, ntt=min(tp, 512), lbb=1, ltt=min(tp, 512),
                 rbb=1, rtt=min(tp, 256), chunk=min(tp, RW_HEAD))
    sb = min(bs, RW_HEAD // ts) if ts == SUBLANES else 1
    cfg_s = dict(tiles, nbb=min(bs, 64), ntt=ts, lbb=min(bs, 32), ltt=ts, rbb=sb, rtt=ts, chunk=ts)
    y_p, st_p = _trunk(x_prompt, mod, bs, p_states, P, cfg_p, "prompt")
    y_s, st_s = _trunk(x_sample, mod, 0, s_states, P, cfg_s, "sample")
    return (y_p, y_s) + st_p + st_s
```

```python
import functools
import math

import jax
import jax.numpy as jnp
from jax import lax
from jax.experimental import pallas as pl
from jax.experimental.pallas import tpu as pltpu

F32 = jnp.float32
BF16 = jnp.bfloat16

LANES = 128
SUBLANES = 8
RW_HEAD = 64
ROWS = 2 * RW_HEAD
LRU_C = 8.0
RMS_EPS = 1e-6
GN_EPS = 64e-5
VMEM_LIMIT = 56 * 1024 * 1024


def _cdiv(a, b):
    return -(-a // b)


def _dot(a, b):
    return jnp.dot(a.astype(BF16), b.astype(BF16), preferred_element_type=F32)


def _dot_nt(a, b):
    return lax.dot_general(a.astype(BF16), b.astype(BF16), (((1,), (1,)), ((), ())),
                           preferred_element_type=F32)


def _dot_tn(a, b):
    return lax.dot_general(a.astype(BF16), b.astype(BF16), (((0,), (0,)), ((), ())),
                           preferred_element_type=F32)


def _dot_hi(a, b):
    n = b.shape[1]
    a_hi, a_lo = _split(a, 2)
    b_hi, b_lo = _split(b, 2)
    lhs = jnp.concatenate([a_hi, a_lo], axis=1)
    rhs = jnp.concatenate([jnp.concatenate([b_hi, b_lo], axis=1),
                           jnp.concatenate([b_hi, jnp.zeros_like(b_lo)], axis=1)], axis=0)
    out = jnp.dot(lhs, rhs, preferred_element_type=F32)
    return out[:, :n] + out[:, n:]


def _split(x, n):
    parts = []
    for _ in range(n - 1):
        p = x.astype(BF16)
        parts.append(p)
        x = x - p.astype(F32)
    parts.append(x.astype(BF16))
    return parts


def _dot_exact_rhs(x, m, n=3):
    return sum(jnp.dot(p, m, preferred_element_type=F32) for p in _split(x, n))


def _dot_exact_lhs(m, x, n=3):
    return sum(jnp.dot(m, p, preferred_element_type=F32) for p in _split(x, n))


def _sigmoid(x):
    return 0.5 * jnp.tanh(0.5 * x) + 0.5


def _softplus(x):
    return jnp.maximum(x, 0.0) + jnp.log1p(jnp.exp(-jnp.abs(x)))


def _silu(x):
    return x * _sigmoid(x)


def _gelu_tanh(x):
    return 0.5 * x * (1.0 + jnp.tanh(math.sqrt(2.0 / math.pi) * (x + 0.044715 * (x * x * x))))


def _mm_resid_kernel(a_ref, w_ref, x_ref, gt_ref, o_ref, wbf_ref):
    @pl.when(pl.program_id(1) == 0)
    def _cast_weights():
        wbf_ref[...] = w_ref[...].astype(BF16)

    acc = jnp.dot(a_ref[...], wbf_ref[...], preferred_element_type=F32)
    nseq = gt_ref.shape[0]
    tm, tn = acc.shape
    out = x_ref[...].reshape(nseq, tm // nseq, tn) + gt_ref[...] * acc.reshape(nseq, tm // nseq, tn)
    o_ref[...] = out.reshape(tm, tn)


def _pick_tile(n, target):
    assert n % LANES == 0
    units = n // LANES
    best = max(u for u in range(1, units + 1) if units % u == 0 and u * LANES <= max(target, LANES))
    return best * LANES


def _gate_spec(layer, gate_col, boff, t_len, tm, tn, d):
    col = lambda j: gate_col * (d // tn) + j
    if tm <= t_len:
        assert t_len % tm == 0
        nt = t_len // tm
        return pl.BlockSpec((None, 1, 1, tn), lambda j, i: (layer, boff + i // nt, 0, col(j)))
    assert tm % t_len == 0 and boff % (tm // t_len) == 0
    nseq = tm // t_len
    return pl.BlockSpec((None, nseq, 1, tn), lambda j, i: (layer, boff // nseq + i, 0, col(j)))


def _matmul_resid(a, w, layer, x, mod, gate_col, boff, t_len, *, tm, tn, name):
    m, k = a.shape
    n = w.shape[-1]
    tm = min(tm, m)
    tn = _pick_tile(n, tn)
    assert m % tm == 0
    return pl.pallas_call(
        _mm_resid_kernel,
        grid=(n // tn, m // tm),
        in_specs=[pl.BlockSpec((tm, k), lambda j, i: (i, 0)),
                  pl.BlockSpec((None, k, tn), lambda j, i: (layer, 0, j)),
                  pl.BlockSpec((tm, tn), lambda j, i: (i, j)),
                  _gate_spec(layer, gate_col, boff, t_len, tm, tn, n)],
        out_specs=pl.BlockSpec((tm, tn), lambda j, i: (i, j)),
        out_shape=jax.ShapeDtypeStruct((m, n), F32),
        scratch_shapes=[pltpu.VMEM((k, tn), BF16)],
        compiler_params=pltpu.CompilerParams(
            dimension_semantics=("arbitrary", "arbitrary"), vmem_limit_bytes=VMEM_LIMIT),
        name=name,
    )(a, w, x, mod)


def _in_proj_kernel(a_ref, wa_ref, wb_ref, o_ref, wbf_ref, *, j0, n_plain, shift):
    j = pl.program_id(0) + j0
    first_row_tile = pl.program_id(1) == 0
    tn = wa_ref.shape[0]

    @pl.when(first_row_tile & (j < n_plain))
    def _cast_weights():
        wbf_ref[...] = wa_ref[...].astype(BF16)

    @pl.when(first_row_tile & (j >= n_plain))
    def _cast_shifted_weights():
        wbf_ref[0:tn - shift, :] = wa_ref[shift:tn, :].astype(BF16)
        if shift:
            wbf_ref[tn - shift:tn, :] = wb_ref[0:shift, :].astype(BF16)

    o_ref[...] = lax.dot_general(a_ref[...], wbf_ref[...], (((1,), (1,)), ((), ())),
                                 preferred_element_type=F32).astype(o_ref.dtype)


def _in_proj(a, w_in_t, layer, *, n_lru, n_rw, n_gates, gates, tm, tn, name):
    m, k = a.shape
    n_in = w_in_t.shape[1]
    tm = min(tm, m)
    tn = _pick_tile(math.gcd(n_lru, n_gates), tn)
    n_rw_t = _cdiv(n_rw, tn) * tn
    gates_start = n_in - n_gates
    n_plain = (n_lru + n_rw_t) // tn
    base, shift = gates_start // tn, gates_start % tn
    bf16_rows = 2 * SUBLANES
    assert m % tm == 0 and n_lru + n_rw_t <= n_in and shift % bf16_rows == 0
    wbw = tn // 2 if (tn // 2) % LANES == 0 and shift <= tn // 2 else tn
    j0, nj = (n_plain, n_gates // tn) if gates else (0, n_plain)
    wa_idx = lambda j: jnp.where(j + j0 < n_plain, j + j0, j + j0 - n_plain + base)
    wb_idx = lambda j: (jnp.maximum(j + j0 - n_plain, 0) + base + (1 if shift else 0)) * (tn // wbw)
    return pl.pallas_call(
        functools.partial(_in_proj_kernel, j0=j0, n_plain=n_plain, shift=shift),
        grid=(nj, m // tm),
        in_specs=[pl.BlockSpec((tm, k), lambda j, i: (i, 0)),
                  pl.BlockSpec((None, tn, k), lambda j, i: (layer, wa_idx(j), 0)),
                  pl.BlockSpec((None, wbw, k), lambda j, i: (layer, wb_idx(j), 0))],
        out_specs=pl.BlockSpec((tm, tn), lambda j, i: (i, j)),
        out_shape=jax.ShapeDtypeStruct((m, nj * tn), BF16 if gates else F32),
        scratch_shapes=[pltpu.VMEM((tn, k), BF16)],
        compiler_params=pltpu.CompilerParams(
            dimension_semantics=("arbitrary", "arbitrary"), vmem_limit_bytes=VMEM_LIMIT),
        name=name,
    )(a, w_in_t, w_in_t)


def _merge_kernel(ga_ref, orw_ref, wpa_ref, wpb_ref, sa_ref, sb_ref, o_ref, wa_bf_ref, wb_bf_ref):
    @pl.when(pl.program_id(1) == 0)
    def _cast_weights():
        wa_bf_ref[...] = wpa_ref[...].astype(BF16)
        wb_bf_ref[...] = wpb_ref[...].astype(BF16)

    y_a = jnp.dot(ga_ref[...], wa_bf_ref[...], preferred_element_type=F32)
    y_b = jnp.dot(orw_ref[...], wb_bf_ref[...], preferred_element_type=F32)
    s_a = _sigmoid(sa_ref[...].astype(F32))
    s_b = _sigmoid(sb_ref[...].astype(F32))
    o_ref[...] = (s_a * y_a + s_b * y_b).astype(o_ref.dtype)


def _merge(ga, o_rw, w_pa, w_pb, gates, layer, *, tm, tn, name):
    m, ca = ga.shape
    cb = o_rw.shape[1]
    d = w_pa.shape[-1]
    tm = min(tm, m)
    tn = _pick_tile(d, tn)
    nd = d // tn
    assert m % tm == 0 and gates.shape == (m, 2 * d)
    return pl.pallas_call(
        _merge_kernel,
        grid=(nd, m // tm),
        in_specs=[pl.BlockSpec((tm, ca), lambda j, i: (i, 0)),
                  pl.BlockSpec((tm, cb), lambda j, i: (i, 0)),
                  pl.BlockSpec((None, ca, tn), lambda j, i: (layer, 0, j)),
                  pl.BlockSpec((None, cb, tn), lambda j, i: (layer, 0, j)),
                  pl.BlockSpec((tm, tn), lambda j, i: (i, j)),
                  pl.BlockSpec((tm, tn), lambda j, i: (i, nd + j))],
        out_specs=pl.BlockSpec((tm, tn), lambda j, i: (i, j)),
        out_shape=jax.ShapeDtypeStruct((m, d), BF16),
        scratch_shapes=[pltpu.VMEM((ca, tn), BF16), pltpu.VMEM((cb, tn), BF16)],
        compiler_params=pltpu.CompilerParams(
            dimension_semantics=("arbitrary", "arbitrary"), vmem_limit_bytes=VMEM_LIMIT),
        name=name,
    )(ga, o_rw, w_pa, w_pb, gates, gates)


def _ffn_up_kernel(a_ref, wg_ref, wv_ref, cwg_ref, cwv_ref, cbg_ref, cbv_ref, bg_ref, bv_ref,
                   o_ref, nbg_ref, nbv_ref, wgb_ref, wvb_ref, xg_ref, xv_ref, *, bb, tt, nt, width):
    i = pl.program_id(1)
    tn = o_ref.shape[-1]
    base = SUBLANES - (width - 1)

    @pl.when(i == 0)
    def _cast_weights():
        wgb_ref[...] = wg_ref[...].astype(BF16)
        wvb_ref[...] = wv_ref[...].astype(BF16)

    @pl.when(i % nt == 0)
    def _sequence_start():
        xg_ref[:, base:SUBLANES, :] = bg_ref[...]
        xv_ref[:, base:SUBLANES, :] = bv_ref[...]

    a = a_ref[...]

    def half(wb_ref, x_ref, cw_ref, cb_ref, nb_ref):
        x_ref[:, SUBLANES:SUBLANES + tt, :] = jnp.dot(
            a, wb_ref[...], preferred_element_type=F32).reshape(bb, tt, tn)
        conv = cb_ref[...] + cw_ref[0:1, :] * x_ref[:, base:base + tt, :]
        for j in range(1, width):
            conv = conv + cw_ref[j:j + 1, :] * x_ref[:, base + j:base + j + tt, :]
        last = x_ref[:, tt + base:tt + SUBLANES, :]
        nb_ref[...] = last
        x_ref[:, base:SUBLANES, :] = last
        return conv

    cg = half(wgb_ref, xg_ref, cwg_ref, cbg_ref, nbg_ref)
    cv = half(wvb_ref, xv_ref, cwv_ref, cbv_ref, nbv_ref)
    o_ref[...] = (_silu(cg) * cv).reshape(bb * tt, tn).astype(o_ref.dtype)


def _ffn_up(a, w, cw, cb, buf, layer, *, t_len, tm, tn, name):
    m, d = a.shape
    f = w.shape[-1] // 2
    width = cw.shape[1]
    bsz = m // t_len
    tm = min(tm, m)
    tn = _pick_tile(f, tn)
    nf = f // tn
    if tm <= t_len:
        assert t_len % tm == 0
        bb, tt = 1, tm
    else:
        assert tm % t_len == 0 and t_len == SUBLANES
        bb, tt = tm // t_len, t_len
    nt = t_len // tt
    assert t_len >= width - 1
    wspec = lambda off: pl.BlockSpec((None, d, tn), lambda j, i: (layer, 0, j + off))
    cspec = lambda rows, off: pl.BlockSpec((None, rows, tn), lambda j, i: (layer, 0, j + off))
    bspec = lambda off: pl.BlockSpec((None, bb, width - 1, tn), lambda j, i: (layer, i // nt, 0, j + off))
    ospec = pl.BlockSpec((bb, width - 1, tn), lambda j, i: (i // nt, 0, j))
    return pl.pallas_call(
        functools.partial(_ffn_up_kernel, bb=bb, tt=tt, nt=nt, width=width),
        grid=(nf, m // tm),
        in_specs=[pl.BlockSpec((tm, d), lambda j, i: (i, 0)), wspec(0), wspec(nf),
                  cspec(width, 0), cspec(width, nf), cspec(1, 0), cspec(1, nf), bspec(0), bspec(nf)],
        out_specs=[pl.BlockSpec((tm, tn), lambda j, i: (i, j)), ospec, ospec],
        out_shape=[jax.ShapeDtypeStruct((m, f), BF16),
                   jax.ShapeDtypeStruct((bsz, width - 1, f), F32),
                   jax.ShapeDtypeStruct((bsz, width - 1, f), F32)],
        scratch_shapes=[pltpu.VMEM((d, tn), BF16), pltpu.VMEM((d, tn), BF16),
                        pltpu.VMEM((bb, tt + SUBLANES, tn), F32), pltpu.VMEM((bb, tt + SUBLANES, tn), F32)],
        compiler_params=pltpu.CompilerParams(
            dimension_semantics=("arbitrary", "arbitrary"), vmem_limit_bytes=VMEM_LIMIT),
        name=name,
    )(a, w, w, cw, cw, cb, cb, buf, buf)


def _norm_mod_kernel(x_ref, g_ref, sc_ref, sh_ref, o_ref):
    x = x_ref[...]
    y = x * lax.rsqrt(jnp.mean(x * x, axis=-1, keepdims=True) + RMS_EPS) * g_ref[...]
    y = y * (1.0 + sc_ref[...]) + sh_ref[...]
    o_ref[...] = y.reshape(o_ref.shape).astype(o_ref.dtype)


def _adaln_kernel(c_ref, w_ref, b_ref, o_ref):
    o_ref[...] = jnp.dot(_silu(c_ref[...]).astype(BF16), w_ref[...].astype(BF16),
                         preferred_element_type=F32) + b_ref[...]


def _adaln(c_all, w_ada, b_ada, *, tn):
    nb, d = c_all.shape
    depth, _, n = w_ada.shape
    tn = _pick_tile(n, tn)
    return pl.pallas_call(
        _adaln_kernel,
        grid=(depth, n // tn),
        in_specs=[pl.BlockSpec((nb, d), lambda l, j: (0, 0)),
                  pl.BlockSpec((None, d, tn), lambda l, j: (l, 0, j)),
                  pl.BlockSpec((None, 1, tn), lambda l, j: (l, 0, j))],
        out_specs=pl.BlockSpec((None, nb, tn), lambda l, j: (l, 0, j)),
        out_shape=jax.ShapeDtypeStruct((depth, nb, n), F32),
        compiler_params=pltpu.CompilerParams(
            dimension_semantics=("arbitrary", "arbitrary"), vmem_limit_bytes=VMEM_LIMIT),
        name="adaln",
    )(c_all, w_ada, b_ada)


def _norm_mod(x, g, layer, mod, sc_idx, sh_idx, boff, *, bb, tt, name):
    b, t, d = x.shape
    nt = t // tt
    assert boff % bb == 0
    return pl.pallas_call(
        _norm_mod_kernel,
        grid=(b // bb, nt),
        in_specs=[pl.BlockSpec((bb, tt, d), lambda i, j: (i, j, 0)),
                  pl.BlockSpec((None, 1, d), lambda i, j: (layer, 0, 0)),
                  pl.BlockSpec((None, bb, 1, d), lambda i, j: (layer, boff // bb + i, 0, sc_idx)),
                  pl.BlockSpec((None, bb, 1, d), lambda i, j: (layer, boff // bb + i, 0, sh_idx))],
        out_specs=pl.BlockSpec((bb * tt, d), lambda i, j: (i * nt + j, 0)),
        out_shape=jax.ShapeDtypeStruct((b * t, d), BF16),
        compiler_params=pltpu.CompilerParams(
            dimension_semantics=("arbitrary", "arbitrary"), vmem_limit_bytes=VMEM_LIMIT),
        name=name,
    )(x, g, mod, mod)


def _final_norm_kernel(x_ref, g_ref, o_ref):
    x = x_ref[...]
    o_ref[...] = x * lax.rsqrt(jnp.mean(x * x, axis=-1, keepdims=True) + RMS_EPS) * g_ref[...]


def _final_norm(x, g, *, bb, tt, name):
    b, t, d = x.shape
    return pl.pallas_call(
        _final_norm_kernel,
        grid=(b // bb, t // tt),
        in_specs=[pl.BlockSpec((bb, tt, d), lambda i, j: (i, j, 0)),
                  pl.BlockSpec((1, d), lambda i, j: (0, 0))],
        out_specs=pl.BlockSpec((bb, tt, d), lambda i, j: (i, j, 0)),
        out_shape=jax.ShapeDtypeStruct((b, t, d), F32),
        compiler_params=pltpu.CompilerParams(
            dimension_semantics=("arbitrary", "arbitrary"), vmem_limit_bytes=VMEM_LIMIT),
        name=name,
    )(x, g)


def _lru_kernel(x_ref, gate_ref, buf_ref, h0_ref, cw_ref, cb_ref, wa_ref, ba_ref, wi_ref, bi_ref,
                lam_ref, ga_ref, nbuf_ref, nh_ref, xx_ref, hc_ref, a_ref, b_ref, h_ref, *, bb, tt, width):
    c = x_ref.shape[-1]
    rows = bb * tt
    halo = width - 1
    base = SUBLANES - halo

    @pl.when(pl.program_id(1) == 0)
    def _init():
        xx_ref[:, base:SUBLANES, :] = buf_ref[...]
        hc_ref[...] = h0_ref[...]

    xx_ref[:, SUBLANES:SUBLANES + tt, :] = x_ref[...].reshape(bb, tt, c)
    conv = cb_ref[...] + cw_ref[0:1, :] * xx_ref[:, base:base + tt, :]
    for j in range(1, width):
        conv = conv + cw_ref[j:j + 1, :] * xx_ref[:, base + j:base + j + tt, :]
    last = xx_ref[:, tt + base:tt + SUBLANES, :]
    nbuf_ref[...] = last
    xx_ref[:, base:SUBLANES, :] = last

    xc = conv.reshape(rows, c)
    xcb = xc.astype(BF16)

    def gate(w_ref, b_ref):
        pre = [jnp.dot(xcb[:, g * LANES:(g + 1) * LANES], w_ref[g], preferred_element_type=F32)
               for g in range(c // LANES)]
        return _sigmoid(jnp.concatenate(pre, axis=1) + b_ref[...])

    r = gate(wa_ref, ba_ref)
    i = gate(wi_ref, bi_ref)
    log_a = -LRU_C * r * _softplus(-lam_ref[...])
    a = jnp.exp(log_a)
    inp = jnp.sqrt(-jnp.tanh(log_a) * (a * a + 1.0)) * (i * xc)

    pos = lax.broadcasted_iota(jnp.int32, (rows, 1), 0) % SUBLANES
    for s in (1, 2, 4):
        a_sh = pltpu.roll(a, s, 0)
        b_sh = pltpu.roll(inp, s, 0)
        m = pos >= s
        inp = jnp.where(m, a * b_sh + inp, inp)
        a = jnp.where(m, a * a_sh, a)

    if tt == SUBLANES:
        h = (inp.reshape(bb, tt, c) + a.reshape(bb, tt, c) * hc_ref[...]).reshape(rows, c)
    else:
        a_ref[...] = a
        b_ref[...] = inp

        def group(g, carry):
            sl = pl.ds(pl.multiple_of(g * SUBLANES, SUBLANES), SUBLANES)
            hg = b_ref[sl, :] + a_ref[sl, :] * carry
            h_ref[sl, :] = hg
            return hg[SUBLANES - 1:SUBLANES, :]

        lax.fori_loop(0, rows // SUBLANES, group, hc_ref[0])
        h = h_ref[...]

    h3 = h.reshape(bb, tt, c)
    hc_ref[...] = h3[:, tt - 1:tt, :]
    nh_ref[...] = h3[:, tt - 1:tt, :]
    ga_ref[...] = (_gelu_tanh(gate_ref[...]) * h).astype(ga_ref.dtype)


def _lru_branch(p_lru, buf, h0, layer, P, *, t, bb, tt, name):
    b = p_lru.shape[0] // t
    c = h0.shape[-1]
    width = P["lru_conv_w"].shape[1]
    nt = t // tt
    assert t >= width - 1 and (tt == SUBLANES or bb == 1)
    lw = lambda shape: pl.BlockSpec((None,) + shape, lambda i, j: (layer,) + (0,) * len(shape))
    kern = functools.partial(_lru_kernel, bb=bb, tt=tt, width=width)
    return pl.pallas_call(
        kern,
        grid=(b // bb, nt),
        in_specs=[pl.BlockSpec((bb * tt, c), lambda i, j: (i * nt + j, 0)),
                  pl.BlockSpec((bb * tt, c), lambda i, j: (i * nt + j, 1)),
                  pl.BlockSpec((None, bb, width - 1, c), lambda i, j: (layer, i, 0, 0)),
                  pl.BlockSpec((None, bb, 1, c), lambda i, j: (layer, i, 0, 0)),
                  lw((width, c)), lw((1, c)), lw((c // LANES, LANES, LANES)), lw((1, c)),
                  lw((c // LANES, LANES, LANES)), lw((1, c)), lw((1, c))],
        out_specs=[pl.BlockSpec((bb * tt, c), lambda i, j: (i * (t // tt) + j, 0)),
                   pl.BlockSpec((bb, width - 1, c), lambda i, j: (i, 0, 0)),
                   pl.BlockSpec((bb, 1, c), lambda i, j: (i, 0, 0))],
        out_shape=[jax.ShapeDtypeStruct((b * t, c), BF16),
                   jax.ShapeDtypeStruct((b, width - 1, c), F32),
                   jax.ShapeDtypeStruct((b, 1, c), F32)],
        scratch_shapes=[pltpu.VMEM((bb, tt + SUBLANES, c), F32),
                        pltpu.VMEM((bb, 1, c), F32),
                        pltpu.VMEM((bb * tt, c), F32),
                        pltpu.VMEM((bb * tt, c), F32),
                        pltpu.VMEM((bb * tt, c), F32)],
        compiler_params=pltpu.CompilerParams(
            dimension_semantics=("arbitrary", "arbitrary"), vmem_limit_bytes=VMEM_LIMIT),
        name=name,
    )(p_lru, p_lru, buf, h0, P["lru_conv_w"], P["lru_conv_b"], P["lru_wa_bd"], P["lru_ba"],
      P["lru_wi_bd"], P["lru_bi"], P["lru_lambda"])


def _rwkv_kernel(xa_ref, xb_ref, sbuf_ref, s0_ref, mu_ref, w0_ref, w2_ref, a0_ref, a2_ref, g2_ref, kkp_ref,
                 ka_ref, rk_ref, gng_ref, gnb_ref, seg_ref, segt_ref, ns_all_ref,
                 o_ref, nshift_ref, ns_ref,
                 sbd_ref, prev_ref, r_ref, kk_ref, km_ref, b_ref, v_ref, lw_ref, y_ref, tmp_ref,
                 *, bb, tt, chunk, dr):
    del ns_all_ref
    rows = bb * tt
    nrw = sbuf_ref.shape[-1]
    pairs = dr // LANES
    gb = RW_HEAD // chunk
    ti = pl.program_id(1)

    @pl.when(ti == 0)
    def _init():
        prev_ref[...] = sbuf_ref[...]
        tmp_ref[...] = jnp.zeros_like(tmp_ref)

        def pack(bi, carry):
            for j in range(pairs):
                sbd_ref[bi, j, 0:RW_HEAD, 0:RW_HEAD] = s0_ref[bi, 2 * j]
                sbd_ref[bi, j, 0:RW_HEAD, RW_HEAD:LANES] = jnp.zeros((RW_HEAD, RW_HEAD), F32)
                tmp_ref[:, 0:RW_HEAD] = s0_ref[bi, 2 * j + 1]
                sbd_ref[bi, j, RW_HEAD:ROWS, :] = pltpu.roll(tmp_ref[...], RW_HEAD, 1)
            return carry

        lax.fori_loop(0, bb, pack, 0)

    x2 = jnp.concatenate([xa_ref[...], xb_ref[:, 0:nrw - xa_ref.shape[-1]]], axis=-1)
    x3 = x2.reshape(bb, tt, nrw)
    rolled = pltpu.roll(x2, 1, 0).reshape(bb, tt, nrw)
    t_pos = lax.broadcasted_iota(jnp.int32, (bb, tt, 1), 1)
    prev3 = jnp.where(t_pos == 0, prev_ref[...], rolled)
    last = x3[:, tt - 1:tt, :]
    prev_ref[...] = last
    nshift_ref[...] = last
    xs = (x3 + (prev3 - x3) * mu_ref[...]).reshape(rows, nrw)

    r = xs[:, 0:dr]
    k = xs[:, dr:2 * dr]
    v = xs[:, 2 * dr:3 * dr]
    lora = xs[:, 3 * dr:]
    log_decay = -math.exp(-0.5) * _sigmoid(w0_ref[...] + _dot(jnp.tanh(lora), w2_ref[...]))
    a = _sigmoid(a0_ref[...] + _dot(lora, a2_ref[...]))
    gg = _dot(_sigmoid(lora), g2_ref[...])

    seg = seg_ref[...]
    segt = segt_ref[...]
    head_sum = lambda z, n=2: _dot_exact_rhs(z, seg, n)
    head_bcast = lambda z: _dot_exact_rhs(z, segt, 2)

    kk = k * kkp_ref[...]
    kk = kk * head_bcast(1.0 / jnp.maximum(jnp.sqrt(head_sum(kk * kk)), 1e-12))
    km = k * (1.0 + (a - 1.0) * ka_ref[...])
    r_ref[...] = r
    kk_ref[...] = kk
    km_ref[...] = km
    b_ref[...] = kk * a
    v_ref[...] = v
    lw_ref[...] = log_decay

    ri = lax.broadcasted_iota(jnp.int32, (ROWS, ROWS), 0)
    ci = lax.broadcasted_iota(jnp.int32, (ROWS, ROWS), 1)
    same = (ri // chunk) == (ci // chunk)
    m_strict = same & ((ci % chunk) < (ri % chunk))
    m_incl = same & ((ci % chunk) <= (ri % chunk))
    m_pair = (ri // RW_HEAD) == (ci // RW_HEAD)
    eye = jnp.where(ri == ci, 1.0, 0.0)
    lvl_masks = []
    s = 1
    while s < chunk:
        lvl_masks.append(((ri // (2 * s)) == (ci // (2 * s))) & ((ri // s) != (ci // s)))
        s *= 2
    r64 = lax.broadcasted_iota(jnp.int32, (RW_HEAD, RW_HEAD), 0)
    c64 = lax.broadcasted_iota(jnp.int32, (RW_HEAD, RW_HEAD), 1)
    tril = jnp.where(((r64 // chunk) == (c64 // chunk)) & (c64 <= r64), 1.0, 0.0).astype(BF16)
    lane_lo = lax.broadcasted_iota(jnp.int32, (1, 1, LANES), 2) < RW_HEAD

    def stack_par(z3):
        return jnp.concatenate([jnp.where(lane_lo, z3, 0.0), jnp.where(lane_lo, 0.0, z3)],
                               axis=1).reshape(ROWS, LANES)

    def stack_dup(z3):
        return jnp.concatenate([z3, z3], axis=1).reshape(ROWS, LANES)

    def row_chunk(rc, carry):
        sl = pl.ds(pl.multiple_of(rc * RW_HEAD, RW_HEAD), RW_HEAD)
        lw = lw_ref[sl, :]
        c_in = _dot_exact_lhs(tril, lw, 3)
        p_in = jnp.exp(c_in)
        p_inv = jnp.exp(-c_in)
        qt = kk_ref[sl, :] * jnp.exp(c_in - lw)
        rt = r_ref[sl, :] * p_in
        kt = km_ref[sl, :] * p_inv
        bt = b_ref[sl, :] * p_inv
        vv = v_ref[sl, :]
        p_end = p_in.reshape(gb, chunk, dr)[:, chunk - 1:chunk, :]
        pr = range(pairs)
        lanes = [slice(j * LANES, (j + 1) * LANES) for j in pr]
        split3 = lambda z: [z[:, ls].reshape(gb, chunk, LANES) for ls in lanes]
        q3, r3, k3, b3, v3 = split3(qt), split3(rt), split3(kt), split3(bt), split3(vv)
        lq = [stack_par(z) for z in q3]
        lr = [stack_par(z) for z in r3]
        vm = [stack_par(z) for z in v3]
        g = [_dot_nt(jnp.concatenate([lq[j], lr[j]], axis=0),
                     jnp.concatenate([stack_dup(b3[j]), stack_dup(k3[j])], axis=0)) for j in pr]
        l_b = [jnp.where(m_strict, g[j][0:ROWS, 0:ROWS], 0.0) for j in pr]
        m_k = [jnp.where(m_strict, g[j][0:ROWS, ROWS:], 0.0) for j in pr]
        n_kb = [jnp.concatenate([jnp.where(m_incl, g[j][ROWS:, ROWS:], 0.0),
                                 jnp.where(m_incl, -g[j][ROWS:, 0:ROWS], 0.0)], axis=1) for j in pr]
        mkv = [_dot(m_k[j], vm[j]) for j in pr]
        t_inv = [eye - jnp.where(lvl_masks[0], l_b[j], 0.0) for j in pr]
        for lm in lvl_masks[1:]:
            w = [_dot(t_inv[j], jnp.where(lm, l_b[j], 0.0)) for j in pr]
            t_inv = [t_inv[j] - _dot(w[j], t_inv[j]) for j in pr]
        bis = [0 if bb == 1 else (rc * RW_HEAD + s * chunk) // tt for s in range(gb)]
        st = [[sbd_ref[bis[s], j] for s in range(gb)] for j in pr]
        qa, ra = [], []
        for j in pr:
            lq3 = lq[j].reshape(gb, 2 * chunk, LANES)
            lr3 = lr[j].reshape(gb, 2 * chunk, LANES)
            qr = [_dot_nt(jnp.concatenate([lq3[s], lr3[s]], axis=0), st[j][s]) for s in range(gb)]
            qa.append(jnp.concatenate([z[0:2 * chunk] for z in qr], axis=0) if gb > 1 else qr[0][0:2 * chunk])
            ra.append(jnp.concatenate([z[2 * chunk:] for z in qr], axis=0) if gb > 1 else qr[0][2 * chunk:])
        x = [_dot_hi(t_inv[j], qa[j] + mkv[j]) for j in pr]
        ys = [ra[j] + _dot(n_kb[j], jnp.concatenate([vm[j], x[j]], axis=0)) for j in pr]
        for j in pr:
            ys3 = ys[j].reshape(gb, 2 * chunk, LANES)
            y_ref[sl, lanes[j]] = (ys3[:, 0:chunk] + ys3[:, chunk:]).reshape(RW_HEAD, LANES)
        for j in pr:
            u3 = x[j].reshape(gb, 2 * chunk, LANES)
            u3 = u3[:, 0:chunk] + u3[:, chunk:]
            for s in range(gb):
                ds = _dot_tn(jnp.concatenate([v3[j][s], -u3[s]], axis=0),
                             jnp.concatenate([k3[j][s], b3[j][s]], axis=0))
                sbd_ref[bis[s], j] = p_end[s][:, lanes[j]] * (st[j][s] + jnp.where(m_pair, ds, 0.0))
        return carry

    lax.fori_loop(0, rows // RW_HEAD, row_chunk, 0)

    y = y_ref[...]
    inv_n = 1.0 / RW_HEAD
    d = y - head_bcast(head_sum(y) * inv_n)
    rstd = lax.rsqrt(head_sum(d * d) * inv_n + GN_EPS)
    gn = d * head_bcast(rstd) * gng_ref[...] + gnb_ref[...]
    bonus = head_bcast(head_sum(r_ref[...] * km_ref[...] * rk_ref[...])) * v_ref[...]
    o_ref[...] = ((gn + bonus) * gg).astype(o_ref.dtype)

    @pl.when(ti == pl.num_programs(1) - 1)
    def _finish():
        def unpack(bi, carry):
            for j in range(pairs):
                ns_ref[bi, 2 * j] = sbd_ref[bi, j, 0:RW_HEAD, 0:RW_HEAD]
                ns_ref[bi, 2 * j + 1] = pltpu.roll(sbd_ref[bi, j, RW_HEAD:ROWS, :], RW_HEAD, 1)[:, 0:RW_HEAD]
            return carry

        lax.fori_loop(0, bb, unpack, 0)


def _rwkv_branch(p_all, rw_col, sbuf, s0, ns_all, layer, P, *, t, bb, tt, chunk, name):
    b = p_all.shape[0] // t
    nt = t // tt
    nrw = sbuf.shape[-1]
    heads = s0.shape[2]
    dr = heads * RW_HEAD
    nl = nrw - 3 * dr
    rows = bb * tt
    xw = rw_col
    assert xw % LANES == 0 and xw < nrw <= 2 * xw and rw_col + 2 * xw <= p_all.shape[-1]
    assert dr % LANES == 0 and RW_HEAD % chunk == 0 and tt % chunk == 0 and rows % RW_HEAD == 0
    assert tt == chunk or bb == 1
    lw = lambda shape: pl.BlockSpec((None,) + shape, lambda i, j: (layer,) + (0,) * len(shape))
    kern = functools.partial(_rwkv_kernel, bb=bb, tt=tt, chunk=chunk, dr=dr)
    big = lambda: pltpu.VMEM((rows, dr), F32)
    args = (p_all, p_all, sbuf, s0, P["rw_mu_p"], P["rw_w0"], P["rw_w2_p"], P["rw_a0"], P["rw_a2_p"], P["rw_g2_p"],
            P["rw_kk"], P["rw_ka"], P["rw_rk"], P["rw_gn_g"], P["rw_gn_b"], P["seg"], P["segt"], ns_all)
    return pl.pallas_call(
        kern,
        grid=(b // bb, t // tt),
        in_specs=[pl.BlockSpec((rows, xw), lambda i, j: (i * nt + j, 1)),
                  pl.BlockSpec((rows, xw), lambda i, j: (i * nt + j, 2)),
                  pl.BlockSpec((None, bb, 1, nrw), lambda i, j: (layer, i, 0, 0)),
                  pl.BlockSpec((None, bb, heads, RW_HEAD, RW_HEAD), lambda i, j: (layer, i, 0, 0, 0)),
                  lw((1, nrw)), lw((1, dr)), lw((nl, dr)), lw((1, dr)), lw((nl, dr)), lw((nl, dr)),
                  lw((1, dr)), lw((1, dr)), lw((1, dr)), lw((1, dr)), lw((1, dr)),
                  pl.BlockSpec((dr, LANES), lambda i, j: (0, 0)),
                  pl.BlockSpec((LANES, dr), lambda i, j: (0, 0)),
                  pl.BlockSpec(memory_space=pl.ANY)],
        out_specs=[pl.BlockSpec((rows, dr), lambda i, j: (i * (t // tt) + j, 0)),
                   pl.BlockSpec((bb, 1, nrw), lambda i, j: (i, 0, 0)),
                   pl.BlockSpec((None, bb, heads, RW_HEAD, RW_HEAD), lambda i, j: (layer, i, 0, 0, 0))],
        out_shape=[jax.ShapeDtypeStruct((b * t, dr), BF16),
                   jax.ShapeDtypeStruct((b, 1, nrw), F32),
                   jax.ShapeDtypeStruct(ns_all.shape, F32)],
        input_output_aliases={len(args) - 1: 2},
        scratch_shapes=[pltpu.VMEM((bb, dr // LANES, ROWS, LANES), F32),
                        pltpu.VMEM((bb, 1, nrw), F32),
                        big(), big(), big(), big(), big(), big(), big(),
                        pltpu.VMEM((RW_HEAD, LANES), F32)],
        compiler_params=pltpu.CompilerParams(
            dimension_semantics=("arbitrary", "arbitrary"), vmem_limit_bytes=VMEM_LIMIT),
        name=name,
    )(*args)


def _trunk(x, mod, boff, states, P, cfg, tag):
    bsz, t_len, d = x.shape
    m = bsz * t_len
    depth = mod.shape[0]
    lru_conv, lru_h, rw_shift, rw_s, ffn_conv = states
    dl = lru_h.shape[-1]
    nrw = P["rw_mu_p"].shape[-1]
    nrw0 = rw_shift.shape[-1]
    outs = ([], [], [], [])
    tm, tn = cfg["tm"], cfg["tn"]
    x = x.reshape(m, d)
    ns_all = jnp.zeros(rw_s.shape, F32)
    lru_h4 = lru_h[:, :, None, :]
    sbuf = jnp.pad(rw_shift, ((0, 0), (0, 0), (0, nrw - nrw0)))[:, :, None, :]
    norm = functools.partial(_norm_mod, bb=cfg["nbb"], tt=cfg["ntt"])
    for l in range(depth):
        h = norm(x.reshape(bsz, t_len, d), P["norm_mix"], l, mod, 1, 0, boff, name=f"norm_mix_{tag}")
        in_proj = functools.partial(_in_proj, h, P["w_in_t"], l, n_lru=2 * dl, n_rw=nrw, n_gates=2 * d, tm=tm, tn=tn)
        p_all = in_proj(gates=False, name=f"in_proj_{tag}")
        p_gates = in_proj(gates=True, name=f"in_gates_{tag}")
        ga, n_lru_buf, n_lru_h = _lru_branch(p_all, lru_conv, lru_h4, l, P, t=t_len,
                                             bb=cfg["lbb"], tt=cfg["ltt"], name=f"lru_{tag}")
        o_rw, n_shift, ns_all = _rwkv_branch(p_all, 2 * dl, sbuf, rw_s, ns_all, l, P, t=t_len, bb=cfg["rbb"],
                                             tt=cfg["rtt"], chunk=cfg["chunk"], name=f"rwkv_{tag}")
        merged = _merge(ga, o_rw, P["w_pa"], P["w_pb"], p_gates, l, tm=tm, tn=cfg["tn_res"], name=f"merge_{tag}")
        x = _matmul_resid(merged, P["w_o"], l, x, mod, 2, boff, t_len, tm=tm, tn=tn, name=f"o_{tag}")

        h2 = norm(x.reshape(bsz, t_len, d), P["norm_ffn"], l, mod, 4, 3, boff, name=f"norm_ffn_{tag}")
        act, n_buf_g, n_buf_v = _ffn_up(h2, P["w_up"], P["ffn_conv_w"], P["ffn_conv_b"], ffn_conv, l,
                                        t_len=t_len, tm=tm, tn=cfg["tn_up"], name=f"ffn_up_{tag}")
        x = _matmul_resid(act, P["w_down"], l, x, mod, 5, boff, t_len, tm=cfg["tm_down"], tn=cfg["tn_res"],
                          name=f"ffn_down_{tag}")

        n_ffn_buf = jnp.concatenate([n_buf_g, n_buf_v], axis=-1)
        for lst, ns in zip(outs, (n_lru_buf, n_lru_h[:, 0, :], n_shift[:, 0, :nrw0], n_ffn_buf)):
            lst.append(ns)
    y = _final_norm(x.reshape(bsz, t_len, d), P["norm_final"], bb=cfg["nbb"], tt=cfg["ntt"], name=f"norm_final_{tag}")
    st = [jnp.stack(lst, axis=0) for lst in outs]
    return y, (st[0], st[1], st[2], ns_all, st[3])


def kernel(x_prompt, x_sample, c_prompt, c_sample, state_lru_conv, state_lru_h, state_rwkv_shift,
           state_rwkv_S, state_ffn_conv, w_ada, b_ada, norm_mix, norm_ffn, w_in, lru_conv_w,
           lru_conv_b, lru_wa, lru_ba, lru_wi, lru_bi, lru_lambda, w_pa, rw_mu, rw_w0, rw_w2, rw_a0,
           rw_a2, rw_g2, rw_kk, rw_ka, rw_rk, rw_gn_g, rw_gn_b, w_pb, w_o, w_up, ffn_conv_w,
           ffn_conv_b, w_down, norm_final):
    depth, d, _ = w_ada.shape
    bp, tp, _ = x_prompt.shape
    bs, ts, _ = x_sample.shape
    dl = lru_lambda.shape[-1]
    nblk, blk = lru_wa.shape[1], lru_wa.shape[2]
    heads = rw_rk.shape[1]
    dr = heads * RW_HEAD
    nrw0 = rw_mu.shape[-1]
    nl = _cdiv(nrw0 - 3 * dr, LANES) * LANES
    nrw = 3 * dr + nl
    lw_n, la_n = rw_w2.shape[1], rw_a2.shape[1]

    row = lambda p: p[:, None, :]
    per = LANES // blk
    eye = jnp.eye(per, dtype=F32)
    block_diag = lambda w: (eye[:, None, :, None] * w.reshape(depth, nblk // per, per, blk, 1, blk)
                            ).reshape(depth, nblk // per, LANES, LANES).astype(BF16)
    pad_rows = lambda w, off: jnp.pad(w, ((0, 0), (off, nl - off - w.shape[1]), (0, 0))).astype(BF16)
    head_of = jnp.arange(dr) // RW_HEAD
    seg = (head_of[:, None] == jnp.arange(LANES)[None, :]).astype(BF16)
    P = dict(
        norm_mix=row(norm_mix), norm_ffn=row(norm_ffn), norm_final=norm_final[None, :],
        w_in_t=jnp.swapaxes(w_in, 1, 2), lru_conv_w=lru_conv_w, lru_conv_b=row(lru_conv_b), lru_wa_bd=block_diag(lru_wa),
        lru_ba=row(lru_ba), lru_wi_bd=block_diag(lru_wi), lru_bi=row(lru_bi), lru_lambda=row(lru_lambda),
        w_pa=w_pa, w_pb=w_pb, w_o=w_o, w_up=w_up, w_down=w_down,
        rw_mu_p=row(jnp.pad(rw_mu, ((0, 0), (0, nrw - nrw0)))),
        rw_w0=row(rw_w0), rw_a0=row(rw_a0),
        rw_w2_p=pad_rows(rw_w2, 0), rw_a2_p=pad_rows(rw_a2, lw_n), rw_g2_p=pad_rows(rw_g2, lw_n + la_n),
        rw_kk=row(rw_kk), rw_ka=row(rw_ka), rw_rk=rw_rk.reshape(depth, 1, dr),
        rw_gn_g=row(rw_gn_g), rw_gn_b=row(rw_gn_b), seg=seg, segt=seg.T,
        ffn_conv_w=ffn_conv_w, ffn_conv_b=row(ffn_conv_b),
    )

    nb = bp + bs
    nb_pad = _cdiv(nb, SUBLANES) * SUBLANES
    c_all = jnp.concatenate([c_sample, c_prompt, jnp.zeros((nb_pad - nb, d), F32)], axis=0)
    mod = _adaln(c_all, w_ada, row(b_ada), tn=1024)[:, :, None, :]

    zeros = lambda *s: jnp.zeros((depth, bp) + s, F32)
    p_states = (zeros(lru_conv_w.shape[1] - 1, dl), zeros(dl), zeros(nrw0),
                zeros(heads, RW_HEAD, RW_HEAD), zeros(ffn_conv_w.shape[1] - 1, w_up.shape[-1]))
    s_states = (state_lru_conv, state_lru_h, state_rwkv_shift, state_rwkv_S, state_ffn_conv)

    tiles = dict(tm=1024, tn=1024, tn_res=512, tn_up=512, tm_down=512)
    cfg_p = dict(tiles, nbb=1, ntt=min(tp, 512), lbb=1, ltt=min(tp, 512),
                 rbb=1, rtt=min(tp, 256), chunk=min(tp, RW_HEAD))
    sb = min(bs, RW_HEAD // ts) if ts == SUBLANES else 1
    cfg_s = dict(tiles, nbb=min(bs, 64), ntt=ts, lbb=min(bs, 32), ltt=ts, rbb=sb, rtt=ts, chunk=ts)
    y_p, st_p = _trunk(x_prompt, mod, bs, p_states, P, cfg_p, "prompt")
    y_s, st_s = _trunk(x_sample, mod, 0, s_states, P, cfg_s, "sample")
    return (y_p, y_s) + st_p + st_s
```

```python
import functools
import math

import jax
import jax.numpy as jnp
from jax import lax
from jax.experimental import pallas as pl
from jax.experimental.pallas import tpu as pltpu

F32 = jnp.float32
BF16 = jnp.bfloat16

LANES = 128
SUBLANES = 8
RW_HEAD = 64
ROWS = 2 * RW_HEAD
LRU_C = 8.0
RMS_EPS = 1e-6
GN_EPS = 64e-5
VMEM_LIMIT = 56 * 1024 * 1024


def _cdiv(a, b):
    return -(-a // b)


def _dot(a, b):
    return jnp.dot(a.astype(BF16), b.astype(BF16), preferred_element_type=F32)


def _dot_nt(a, b):
    return lax.dot_general(a.astype(BF16), b.astype(BF16), (((1,), (1,)), ((), ())),
                           preferred_element_type=F32)


def _dot_tn(a, b):
    return lax.dot_general(a.astype(BF16), b.astype(BF16), (((0,), (0,)), ((), ())),
                           preferred_element_type=F32)


def _dot_hi(a, b):
    n = b.shape[1]
    a_hi, a_lo = _split(a, 2)
    b_hi, b_lo = _split(b, 2)
    lhs = jnp.concatenate([a_hi, a_lo], axis=1)
    rhs = jnp.concatenate([jnp.concatenate([b_hi, b_lo], axis=1),
                           jnp.concatenate([b_hi, jnp.zeros_like(b_lo)], axis=1)], axis=0)
    out = jnp.dot(lhs, rhs, preferred_element_type=F32)
    return out[:, :n] + out[:, n:]


def _split(x, n):
    parts = []
    for _ in range(n - 1):
        p = x.astype(BF16)
        parts.append(p)
        x = x - p.astype(F32)
    parts.append(x.astype(BF16))
    return parts


def _dot_exact_rhs(x, m, n=3):
    return sum(jnp.dot(p, m, preferred_element_type=F32) for p in _split(x, n))


def _dot_exact_lhs(m, x, n=3):
    return sum(jnp.dot(m, p, preferred_element_type=F32) for p in _split(x, n))


def _sigmoid(x):
    return 0.5 * jnp.tanh(0.5 * x) + 0.5


def _softplus(x):
    return jnp.maximum(x, 0.0) + jnp.log1p(jnp.exp(-jnp.abs(x)))


def _silu(x):
    return x * _sigmoid(x)


def _gelu_tanh(x):
    return 0.5 * x * (1.0 + jnp.tanh(math.sqrt(2.0 / math.pi) * (x + 0.044715 * (x * x * x))))


def _mm_resid_kernel(a_ref, w_ref, x_ref, gt_ref, o_ref, wbf_ref):
    @pl.when(pl.program_id(1) == 0)
    def _cast_weights():
        wbf_ref[...] = w_ref[...].astype(BF16)

    acc = jnp.dot(a_ref[...], wbf_ref[...], preferred_element_type=F32)
    nseq = gt_ref.shape[0]
    tm, tn = acc.shape
    out = x_ref[...].reshape(nseq, tm // nseq, tn) + gt_ref[...] * acc.reshape(nseq, tm // nseq, tn)
    o_ref[...] = out.reshape(tm, tn)


def _pick_tile(n, target):
    assert n % LANES == 0
    units = n // LANES
    best = max(u for u in range(1, units + 1) if units % u == 0 and u * LANES <= max(target, LANES))
    return best * LANES


def _gate_spec(layer, gate_col, boff, t_len, tm, tn, d):
    col = lambda j: gate_col * (d // tn) + j
    if tm <= t_len:
        assert t_len % tm == 0
        nt = t_len // tm
        return pl.BlockSpec((None, 1, 1, tn), lambda j, i: (layer, boff + i // nt, 0, col(j)))
    assert tm % t_len == 0 and boff % (tm // t_len) == 0
    nseq = tm // t_len
    return pl.BlockSpec((None, nseq, 1, tn), lambda j, i: (layer, boff // nseq + i, 0, col(j)))


def _matmul_resid(a, w, layer, x, mod, gate_col, boff, t_len, *, tm, tn, name):
    m, k = a.shape
    n = w.shape[-1]
    tm = min(tm, m)
    tn = _pick_tile(n, tn)
    assert m % tm == 0
    return pl.pallas_call(
        _mm_resid_kernel,
        grid=(n // tn, m // tm),
        in_specs=[pl.BlockSpec((tm, k), lambda j, i: (i, 0)),
                  pl.BlockSpec((None, k, tn), lambda j, i: (layer, 0, j)),
                  pl.BlockSpec((tm, tn), lambda j, i: (i, j)),
                  _gate_spec(layer, gate_col, boff, t_len, tm, tn, n)],
        out_specs=pl.BlockSpec((tm, tn), lambda j, i: (i, j)),
        out_shape=jax.ShapeDtypeStruct((m, n), F32),
        scratch_shapes=[pltpu.VMEM((k, tn), BF16)],
        compiler_params=pltpu.CompilerParams(
            dimension_semantics=("arbitrary", "arbitrary"), vmem_limit_bytes=VMEM_LIMIT),
        name=name,
    )(a, w, x, mod)


def _in_proj_kernel(a_ref, wa_ref, wb_ref, o_ref, wbf_ref, *, j0, n_plain, shift):
    j = pl.program_id(0) + j0
    first_row_tile = pl.program_id(1) == 0
    tn = wa_ref.shape[0]

    @pl.when(first_row_tile & (j < n_plain))
    def _cast_weights():
        wbf_ref[...] = wa_ref[...].astype(BF16)

    @pl.when(first_row_tile & (j >= n_plain))
    def _cast_shifted_weights():
        wbf_ref[0:tn - shift, :] = wa_ref[shift:tn, :].astype(BF16)
        if shift:
            wbf_ref[tn - shift:tn, :] = wb_ref[0:shift, :].astype(BF16)

    o_ref[...] = lax.dot_general(a_ref[...], wbf_ref[...], (((1,), (1,)), ((), ())),
                                 preferred_element_type=F32).astype(o_ref.dtype)


def _in_proj(a, w_in_t, layer, *, n_lru, n_rw, n_gates, gates, tm, tn, name):
    m, k = a.shape
    n_in = w_in_t.shape[1]
    tm = min(tm, m)
    tn = _pick_tile(math.gcd(n_lru, n_gates), tn)
    n_rw_t = _cdiv(n_rw, tn) * tn
    gates_start = n_in - n_gates
    n_plain = (n_lru + n_rw_t) // tn
    base, shift = gates_start // tn, gates_start % tn
    bf16_rows = 2 * SUBLANES
    assert m % tm == 0 and n_lru + n_rw_t <= n_in and shift % bf16_rows == 0
    wbw = tn // 2 if (tn // 2) % LANES == 0 and shift <= tn // 2 else tn
    j0, nj = (n_plain, n_gates // tn) if gates else (0, n_plain)
    wa_idx = lambda j: jnp.where(j + j0 < n_plain, j + j0, j + j0 - n_plain + base)
    wb_idx = lambda j: (jnp.maximum(j + j0 - n_plain, 0) + base + (1 if shift else 0)) * (tn // wbw)
    return pl.pallas_call(
        functools.partial(_in_proj_kernel, j0=j0, n_plain=n_plain, shift=shift),
        grid=(nj, m // tm),
        in_specs=[pl.BlockSpec((tm, k), lambda j, i: (i, 0)),
                  pl.BlockSpec((None, tn, k), lambda j, i: (layer, wa_idx(j), 0)),
                  pl.BlockSpec((None, wbw, k), lambda j, i: (layer, wb_idx(j), 0))],
        out_specs=pl.BlockSpec((tm, tn), lambda j, i: (i, j)),
        out_shape=jax.ShapeDtypeStruct((m, nj * tn), BF16 if gates else F32),
        scratch_shapes=[pltpu.VMEM((tn, k), BF16)],
        compiler_params=pltpu.CompilerParams(
            dimension_semantics=("arbitrary", "arbitrary"), vmem_limit_bytes=VMEM_LIMIT),
        name=name,
    )(a, w_in_t, w_in_t)


def _merge_kernel(ga_ref, orw_ref, wpa_ref, wpb_ref, sa_ref, sb_ref, o_ref, wa_bf_ref, wb_bf_ref):
    @pl.when(pl.program_id(1) == 0)
    def _cast_weights():
        wa_bf_ref[...] = wpa_ref[...].astype(BF16)
        wb_bf_ref[...] = wpb_ref[...].astype(BF16)

    y_a = jnp.dot(ga_ref[...], wa_bf_ref[...], preferred_element_type=F32)
    y_b = jnp.dot(orw_ref[...], wb_bf_ref[...], preferred_element_type=F32)
    s_a = _sigmoid(sa_ref[...].astype(F32))
    s_b = _sigmoid(sb_ref[...].astype(F32))
    o_ref[...] = (s_a * y_a + s_b * y_b).astype(o_ref.dtype)


def _merge(ga, o_rw, w_pa, w_pb, gates, layer, *, tm, tn, name):
    m, ca = ga.shape
    cb = o_rw.shape[1]
    d = w_pa.shape[-1]
    tm = min(tm, m)
    tn = _pick_tile(d, tn)
    nd = d // tn
    assert m % tm == 0 and gates.shape == (m, 2 * d)
    return pl.pallas_call(
        _merge_kernel,
        grid=(nd, m // tm),
        in_specs=[pl.BlockSpec((tm, ca), lambda j, i: (i, 0)),
                  pl.BlockSpec((tm, cb), lambda j, i: (i, 0)),
                  pl.BlockSpec((None, ca, tn), lambda j, i: (layer, 0, j)),
                  pl.BlockSpec((None, cb, tn), lambda j, i: (layer, 0, j)),
                  pl.BlockSpec((tm, tn), lambda j, i: (i, j)),
                  pl.BlockSpec((tm, tn), lambda j, i: (i, nd + j))],
        out_specs=pl.BlockSpec((tm, tn), lambda j, i: (i, j)),
        out_shape=jax.ShapeDtypeStruct((m, d), BF16),
        scratch_shapes=[pltpu.VMEM((ca, tn), BF16), pltpu.VMEM((cb, tn), BF16)],
        compiler_params=pltpu.CompilerParams(
            dimension_semantics=("arbitrary", "arbitrary"), vmem_limit_bytes=VMEM_LIMIT),
        name=name,
    )(ga, o_rw, w_pa, w_pb, gates, gates)


def _ffn_up_kernel(a_ref, wg_ref, wv_ref, cwg_ref, cwv_ref, cbg_ref, cbv_ref, bg_ref, bv_ref,
                   o_ref, nbg_ref, nbv_ref, wgb_ref, wvb_ref, xg_ref, xv_ref, *, bb, tt, nt, width):
    i = pl.program_id(1)
    tn = o_ref.shape[-1]
    base = SUBLANES - (width - 1)

    @pl.when(i == 0)
    def _cast_weights():
        wgb_ref[...] = wg_ref[...].astype(BF16)
        wvb_ref[...] = wv_ref[...].astype(BF16)

    @pl.when(i % nt == 0)
    def _sequence_start():
        xg_ref[:, base:SUBLANES, :] = bg_ref[...]
        xv_ref[:, base:SUBLANES, :] = bv_ref[...]

    a = a_ref[...]

    def half(wb_ref, x_ref, cw_ref, cb_ref, nb_ref):
        x_ref[:, SUBLANES:SUBLANES + tt, :] = jnp.dot(
            a, wb_ref[...], preferred_element_type=F32).reshape(bb, tt, tn)
        conv = cb_ref[...] + cw_ref[0:1, :] * x_ref[:, base:base + tt, :]
        for j in range(1, width):
            conv = conv + cw_ref[j:j + 1, :] * x_ref[:, base + j:base + j + tt, :]
        last = x_ref[:, tt + base:tt + SUBLANES, :]
        nb_ref[...] = last
        x_ref[:, base:SUBLANES, :] = last
        return conv

    cg = half(wgb_ref, xg_ref, cwg_ref, cbg_ref, nbg_ref)
    cv = half(wvb_ref, xv_ref, cwv_ref, cbv_ref, nbv_ref)
    o_ref[...] = (_silu(cg) * cv).reshape(bb * tt, tn).astype(o_ref.dtype)


def _ffn_up(a, w, cw, cb, buf, layer, *, t_len, tm, tn, name):
    m, d = a.shape
    f = w.shape[-1] // 2
    width = cw.shape[1]
    bsz = m // t_len
    tm = min(tm, m)
    tn = _pick_tile(f, tn)
    nf = f // tn
    if tm <= t_len:
        assert t_len % tm == 0
        bb, tt = 1, tm
    else:
        assert tm % t_len == 0 and t_len == SUBLANES
        bb, tt = tm // t_len, t_len
    nt = t_len // tt
    assert t_len >= width - 1
    wspec = lambda off: pl.BlockSpec((None, d, tn), lambda j, i: (layer, 0, j + off))
    cspec = lambda rows, off: pl.BlockSpec((None, rows, tn), lambda j, i: (layer, 0, j + off))
    bspec = lambda off: pl.BlockSpec((None, bb, width - 1, tn), lambda j, i: (layer, i // nt, 0, j + off))
    ospec = pl.BlockSpec((bb, width - 1, tn), lambda j, i: (i // nt, 0, j))
    return pl.pallas_call(
        functools.partial(_ffn_up_kernel, bb=bb, tt=tt, nt=nt, width=width),
        grid=(nf, m // tm),
        in_specs=[pl.BlockSpec((tm, d), lambda j, i: (i, 0)), wspec(0), wspec(nf),
                  cspec(width, 0), cspec(width, nf), cspec(1, 0), cspec(1, nf), bspec(0), bspec(nf)],
        out_specs=[pl.BlockSpec((tm, tn), lambda j, i: (i, j)), ospec, ospec],
        out_shape=[jax.ShapeDtypeStruct((m, f), BF16),
                   jax.ShapeDtypeStruct((bsz, width - 1, f), F32),
                   jax.ShapeDtypeStruct((bsz, width - 1, f), F32)],
        scratch_shapes=[pltpu.VMEM((d, tn), BF16), pltpu.VMEM((d, tn), BF16),
                        pltpu.VMEM((bb, tt + SUBLANES, tn), F32), pltpu.VMEM((bb, tt + SUBLANES, tn), F32)],
        compiler_params=pltpu.CompilerParams(
            dimension_semantics=("arbitrary", "arbitrary"), vmem_limit_bytes=VMEM_LIMIT),
        name=name,
    )(a, w, w, cw, cw, cb, cb, buf, buf)


def _norm_mod_kernel(x_ref, g_ref, sc_ref, sh_ref, o_ref):
    x = x_ref[...]
    y = x * lax.rsqrt(jnp.mean(x * x, axis=-1, keepdims=True) + RMS_EPS) * g_ref[...]
    y = y * (1.0 + sc_ref[...]) + sh_ref[...]
    o_ref[...] = y.reshape(o_ref.shape).astype(o_ref.dtype)


def _adaln_kernel(c_ref, w_ref, b_ref, o_ref):
    o_ref[...] = jnp.dot(_silu(c_ref[...]).astype(BF16), w_ref[...].astype(BF16),
                         preferred_element_type=F32) + b_ref[...]


def _adaln(c_all, w_ada, b_ada, *, tn):
    nb, d = c_all.shape
    depth, _, n = w_ada.shape
    tn = _pick_tile(n, tn)
    return pl.pallas_call(
        _adaln_kernel,
        grid=(depth, n // tn),
        in_specs=[pl.BlockSpec((nb, d), lambda l, j: (0, 0)),
                  pl.BlockSpec((None, d, tn), lambda l, j: (l, 0, j)),
                  pl.BlockSpec((None, 1, tn), lambda l, j: (l, 0, j))],
        out_specs=pl.BlockSpec((None, nb, tn), lambda l, j: (l, 0, j)),
        out_shape=jax.ShapeDtypeStruct((depth, nb, n), F32),
        compiler_params=pltpu.CompilerParams(
            dimension_semantics=("arbitrary", "arbitrary"), vmem_limit_bytes=VMEM_LIMIT),
        name="adaln",
    )(c_all, w_ada, b_ada)


def _norm_mod(x, g, layer, mod, sc_idx, sh_idx, boff, *, bb, tt, name):
    b, t, d = x.shape
    nt = t // tt
    assert boff % bb == 0
    return pl.pallas_call(
        _norm_mod_kernel,
        grid=(b // bb, nt),
        in_specs=[pl.BlockSpec((bb, tt, d), lambda i, j: (i, j, 0)),
                  pl.BlockSpec((None, 1, d), lambda i, j: (layer, 0, 0)),
                  pl.BlockSpec((None, bb, 1, d), lambda i, j: (layer, boff // bb + i, 0, sc_idx)),
                  pl.BlockSpec((None, bb, 1, d), lambda i, j: (layer, boff // bb + i, 0, sh_idx))],
        out_specs=pl.BlockSpec((bb * tt, d), lambda i, j: (i * nt + j, 0)),
        out_shape=jax.ShapeDtypeStruct((b * t, d), BF16),
        compiler_params=pltpu.CompilerParams(
            dimension_semantics=("arbitrary", "arbitrary"), vmem_limit_bytes=VMEM_LIMIT),
        name=name,
    )(x, g, mod, mod)


def _final_norm_kernel(x_ref, g_ref, o_ref):
    x = x_ref[...]
    o_ref[...] = x * lax.rsqrt(jnp.mean(x * x, axis=-1, keepdims=True) + RMS_EPS) * g_ref[...]


def _final_norm(x, g, *, bb, tt, name):
    b, t, d = x.shape
    return pl.pallas_call(
        _final_norm_kernel,
        grid=(b // bb, t // tt),
        in_specs=[pl.BlockSpec((bb, tt, d), lambda i, j: (i, j, 0)),
                  pl.BlockSpec((1, d), lambda i, j: (0, 0))],
        out_specs=pl.BlockSpec((bb, tt, d), lambda i, j: (i, j, 0)),
        out_shape=jax.ShapeDtypeStruct((b, t, d), F32),
        compiler_params=pltpu.CompilerParams(
            dimension_semantics=("arbitrary", "arbitrary"), vmem_limit_bytes=VMEM_LIMIT),
        name=name,
    )(x, g)


def _lru_kernel(x_ref, gate_ref, buf_ref, h0_ref, cw_ref, cb_ref, wa_ref, ba_ref, wi_ref, bi_ref,
                lam_ref, ga_ref, nbuf_ref, nh_ref, xx_ref, hc_ref, a_ref, b_ref, h_ref, *, bb, tt, width):
    c = x_ref.shape[-1]
    rows = bb * tt
    halo = width - 1
    base = SUBLANES - halo

    @pl.when(pl.program_id(1) == 0)
    def _init():
        xx_ref[:, base:SUBLANES, :] = buf_ref[...]
        hc_ref[...] = h0_ref[...]

    xx_ref[:, SUBLANES:SUBLANES + tt, :] = x_ref[...].reshape(bb, tt, c)
    conv = cb_ref[...] + cw_ref[0:1, :] * xx_ref[:, base:base + tt, :]
    for j in range(1, width):
        conv = conv + cw_ref[j:j + 1, :] * xx_ref[:, base + j:base + j + tt, :]
    last = xx_ref[:, tt + base:tt + SUBLANES, :]
    nbuf_ref[...] = last
    xx_ref[:, base:SUBLANES, :] = last

    xc = conv.reshape(rows, c)
    xcb = xc.astype(BF16)

    def gate(w_ref, b_ref):
        pre = [jnp.dot(xcb[:, g * LANES:(g + 1) * LANES], w_ref[g], preferred_element_type=F32)
               for g in range(c // LANES)]
        return _sigmoid(jnp.concatenate(pre, axis=1) + b_ref[...])

    r = gate(wa_ref, ba_ref)
    i = gate(wi_ref, bi_ref)
    log_a = -LRU_C * r * _softplus(-lam_ref[...])
    a = jnp.exp(log_a)
    inp = jnp.sqrt(-jnp.tanh(log_a) * (a * a + 1.0)) * (i * xc)

    pos = lax.broadcasted_iota(jnp.int32, (rows, 1), 0) % SUBLANES
    for s in (1, 2, 4):
        a_sh = pltpu.roll(a, s, 0)
        b_sh = pltpu.roll(inp, s, 0)
        m = pos >= s
        inp = jnp.where(m, a * b_sh + inp, inp)
        a = jnp.where(m, a * a_sh, a)

    if tt == SUBLANES:
        h = (inp.reshape(bb, tt, c) + a.reshape(bb, tt, c) * hc_ref[...]).reshape(rows, c)
    else:
        a_ref[...] = a
        b_ref[...] = inp

        def group(g, carry):
            sl = pl.ds(pl.multiple_of(g * SUBLANES, SUBLANES), SUBLANES)
            hg = b_ref[sl, :] + a_ref[sl, :] * carry
            h_ref[sl, :] = hg
            return hg[SUBLANES - 1:SUBLANES, :]

        lax.fori_loop(0, rows // SUBLANES, group, hc_ref[0])
        h = h_ref[...]

    h3 = h.reshape(bb, tt, c)
    hc_ref[...] = h3[:, tt - 1:tt, :]
    nh_ref[...] = h3[:, tt - 1:tt, :]
    ga_ref[...] = (_gelu_tanh(gate_ref[...]) * h).astype(ga_ref.dtype)


def _lru_branch(p_lru, buf, h0, layer, P, *, t, bb, tt, name):
    b = p_lru.shape[0] // t
    c = h0.shape[-1]
    width = P["lru_conv_w"].shape[1]
    nt = t // tt
    assert t >= width - 1 and (tt == SUBLANES or bb == 1)
    lw = lambda shape: pl.BlockSpec((None,) + shape, lambda i, j: (layer,) + (0,) * len(shape))
    kern = functools.partial(_lru_kernel, bb=bb, tt=tt, width=width)
    return pl.pallas_call(
        kern,
        grid=(b // bb, nt),
        in_specs=[pl.BlockSpec((bb * tt, c), lambda i, j: (i * nt + j, 0)),
                  pl.BlockSpec((bb * tt, c), lambda i, j: (i * nt + j, 1)),
                  pl.BlockSpec((None, bb, width - 1, c), lambda i, j: (layer, i, 0, 0)),
                  pl.BlockSpec((None, bb, 1, c), lambda i, j: (layer, i, 0, 0)),
                  lw((width, c)), lw((1, c)), lw((c // LANES, LANES, LANES)), lw((1, c)),
                  lw((c // LANES, LANES, LANES)), lw((1, c)), lw((1, c))],
        out_specs=[pl.BlockSpec((bb * tt, c), lambda i, j: (i * (t // tt) + j, 0)),
                   pl.BlockSpec((bb, width - 1, c), lambda i, j: (i, 0, 0)),
                   pl.BlockSpec((bb, 1, c), lambda i, j: (i, 0, 0))],
        out_shape=[jax.ShapeDtypeStruct((b * t, c), BF16),
                   jax.ShapeDtypeStruct((b, width - 1, c), F32),
                   jax.ShapeDtypeStruct((b, 1, c), F32)],
        scratch_shapes=[pltpu.VMEM((bb, tt + SUBLANES, c), F32),
                        pltpu.VMEM((bb, 1, c), F32),
                        pltpu.VMEM((bb * tt, c), F32),
                        pltpu.VMEM((bb * tt, c), F32),
                        pltpu.VMEM((bb * tt, c), F32)],
        compiler_params=pltpu.CompilerParams(
            dimension_semantics=("arbitrary", "arbitrary"), vmem_limit_bytes=VMEM_LIMIT),
        name=name,
    )(p_lru, p_lru, buf, h0, P["lru_conv_w"], P["lru_conv_b"], P["lru_wa_bd"], P["lru_ba"],
      P["lru_wi_bd"], P["lru_bi"], P["lru_lambda"])


def _rwkv_kernel(xa_ref, xb_ref, sbuf_ref, s0_ref, mu_ref, w0_ref, w2_ref, a0_ref, a2_ref, g2_ref, kkp_ref,
                 ka_ref, rk_ref, gng_ref, gnb_ref, seg_ref, segt_ref, ns_all_ref,
                 o_ref, nshift_ref, ns_ref,
                 sbd_ref, prev_ref, r_ref, kk_ref, km_ref, b_ref, v_ref, lw_ref, y_ref, tmp_ref,
                 *, bb, tt, chunk, dr):
    del ns_all_ref
    rows = bb * tt
    nrw = sbuf_ref.shape[-1]
    pairs = dr // LANES
    gb = RW_HEAD // chunk
    ti = pl.program_id(1)

    @pl.when(ti == 0)
    def _init():
        prev_ref[...] = sbuf_ref[...]
        tmp_ref[...] = jnp.zeros_like(tmp_ref)

        def pack(bi, carry):
            for j in range(pairs):
                sbd_ref[bi, j, 0:RW_HEAD, 0:RW_HEAD] = s0_ref[bi, 2 * j]
                sbd_ref[bi, j, 0:RW_HEAD, RW_HEAD:LANES] = jnp.zeros((RW_HEAD, RW_HEAD), F32)
                tmp_ref[:, 0:RW_HEAD] = s0_ref[bi, 2 * j + 1]
                sbd_ref[bi, j, RW_HEAD:ROWS, :] = pltpu.roll(tmp_ref[...], RW_HEAD, 1)
            return carry

        lax.fori_loop(0, bb, pack, 0)

    x2 = jnp.concatenate([xa_ref[...], xb_ref[:, 0:nrw - xa_ref.shape[-1]]], axis=-1)
    x3 = x2.reshape(bb, tt, nrw)
    rolled = pltpu.roll(x2, 1, 0).reshape(bb, tt, nrw)
    t_pos = lax.broadcasted_iota(jnp.int32, (bb, tt, 1), 1)
    prev3 = jnp.where(t_pos == 0, prev_ref[...], rolled)
    last = x3[:, tt - 1:tt, :]
    prev_ref[...] = last
    nshift_ref[...] = last
    xs = (x3 + (prev3 - x3) * mu_ref[...]).reshape(rows, nrw)

    r = xs[:, 0:dr]
    k = xs[:, dr:2 * dr]
    v = xs[:, 2 * dr:3 * dr]
    lora = xs[:, 3 * dr:]
    log_decay = -math.exp(-0.5) * _sigmoid(w0_ref[...] + _dot(jnp.tanh(lora), w2_ref[...]))
    a = _sigmoid(a0_ref[...] + _dot(lora, a2_ref[...]))
    gg = _dot(_sigmoid(lora), g2_ref[...])

    seg = seg_ref[...]
    segt = segt_ref[...]
    head_sum = lambda z, n=2: _dot_exact_rhs(z, seg, n)
    head_bcast = lambda z: _dot_exact_rhs(z, segt, 2)

    kk = k * kkp_ref[...]
    kk = kk * head_bcast(1.0 / jnp.maximum(jnp.sqrt(head_sum(kk * kk)), 1e-12))
    km = k * (1.0 + (a - 1.0) * ka_ref[...])
    r_ref[...] = r
    kk_ref[...] = kk
    km_ref[...] = km
    b_ref[...] = kk * a
    v_ref[...] = v
    lw_ref[...] = log_decay

    ri = lax.broadcasted_iota(jnp.int32, (ROWS, ROWS), 0)
    ci = lax.broadcasted_iota(jnp.int32, (ROWS, ROWS), 1)
    same = (ri // chunk) == (ci // chunk)
    m_strict = same & ((ci % chunk) < (ri % chunk))
    m_incl = same & ((ci % chunk) <= (ri % chunk))
    m_pair = (ri // RW_HEAD) == (ci // RW_HEAD)
    eye = jnp.where(ri == ci, 1.0, 0.0)
    lvl_masks = []
    s = 1
    while s < chunk:
        lvl_masks.append(((ri // (2 * s)) == (ci // (2 * s))) & ((ri // s) != (ci // s)))
        s *= 2
    r64 = lax.broadcasted_iota(jnp.int32, (RW_HEAD, RW_HEAD), 0)
    c64 = lax.broadcasted_iota(jnp.int32, (RW_HEAD, RW_HEAD), 1)
    tril = jnp.where(((r64 // chunk) == (c64 // chunk)) & (c64 <= r64), 1.0, 0.0).astype(BF16)
    lane_lo = lax.broadcasted_iota(jnp.int32, (1, 1, LANES), 2) < RW_HEAD

    def stack_par(z3):
        return jnp.concatenate([jnp.where(lane_lo, z3, 0.0), jnp.where(lane_lo, 0.0, z3)],
                               axis=1).reshape(ROWS, LANES)

    def stack_dup(z3):
        return jnp.concatenate([z3, z3], axis=1).reshape(ROWS, LANES)

    pr = range(pairs)
    lanes = [slice(j * LANES, (j + 1) * LANES) for j in pr]
    n_chunks = rows // RW_HEAD
    per_trip = 2 if n_chunks % 2 == 0 else 1
    prob = [(u, j) for u in range(per_trip) for j in pr]

    def row_chunks(it, carry):
        rcs = [it * per_trip + u for u in range(per_trip)]
        sls = [pl.ds(pl.multiple_of(rc * RW_HEAD, RW_HEAD), RW_HEAD) for rc in rcs]
        q3, r3, k3, b3, v3, p_end = {}, {}, {}, {}, {}, []
        for u, sl in enumerate(sls):
            lw = lw_ref[sl, :]
            c_in = _dot_exact_lhs(tril, lw, 3)
            p_in = jnp.exp(c_in)
            p_inv = jnp.exp(-c_in)
            slabs = (kk_ref[sl, :] * jnp.exp(c_in - lw), r_ref[sl, :] * p_in, km_ref[sl, :] * p_inv,
                     b_ref[sl, :] * p_inv, v_ref[sl, :])
            p_end.append(p_in.reshape(gb, chunk, dr)[:, chunk - 1:chunk, :])
            for j in pr:
                q3[u, j], r3[u, j], k3[u, j], b3[u, j], v3[u, j] = (
                    z[:, lanes[j]].reshape(gb, chunk, LANES) for z in slabs)
        lq = {p: stack_par(q3[p]) for p in prob}
        lr = {p: stack_par(r3[p]) for p in prob}
        vm = {p: stack_par(v3[p]) for p in prob}
        g = {p: _dot_nt(jnp.concatenate([lq[p], lr[p]], axis=0),
                        jnp.concatenate([stack_dup(b3[p]), stack_dup(k3[p])], axis=0)) for p in prob}
        l_b = {p: jnp.where(m_strict, g[p][0:ROWS, 0:ROWS], 0.0) for p in prob}
        n_kb = {p: jnp.concatenate([jnp.where(m_incl, g[p][ROWS:, ROWS:], 0.0),
                                    jnp.where(m_incl, -g[p][ROWS:, 0:ROWS], 0.0)], axis=1) for p in prob}
        mkv = {p: _dot(jnp.where(m_strict, g[p][0:ROWS, ROWS:], 0.0), vm[p]) for p in prob}
        t_inv = {p: eye - jnp.where(lvl_masks[0], l_b[p], 0.0) for p in prob}
        for lm in lvl_masks[1:]:
            w = {p: _dot(t_inv[p], jnp.where(lm, l_b[p], 0.0)) for p in prob}
            t_inv = {p: t_inv[p] - _dot(w[p], t_inv[p]) for p in prob}
        for u, (rc, sl) in enumerate(zip(rcs, sls)):
            bis = [0 if bb == 1 else (rc * RW_HEAD + s * chunk) // tt for s in range(gb)]
            st = [[sbd_ref[bis[s], j] for s in range(gb)] for j in pr]
            qa, ra = [], []
            for j in pr:
                lq3 = lq[u, j].reshape(gb, 2 * chunk, LANES)
                lr3 = lr[u, j].reshape(gb, 2 * chunk, LANES)
                qr = [_dot_nt(jnp.concatenate([lq3[s], lr3[s]], axis=0), st[j][s]) for s in range(gb)]
                qa.append(jnp.concatenate([z[0:2 * chunk] for z in qr], axis=0) if gb > 1 else qr[0][0:2 * chunk])
                ra.append(jnp.concatenate([z[2 * chunk:] for z in qr], axis=0) if gb > 1 else qr[0][2 * chunk:])
            x = [_dot_hi(t_inv[u, j], qa[j] + mkv[u, j]) for j in pr]
            ys = [ra[j] + _dot(n_kb[u, j], jnp.concatenate([vm[u, j], x[j]], axis=0)) for j in pr]
            for j in pr:
                ys3 = ys[j].reshape(gb, 2 * chunk, LANES)
                y_ref[sl, lanes[j]] = (ys3[:, 0:chunk] + ys3[:, chunk:]).reshape(RW_HEAD, LANES)
            for j in pr:
                u3 = x[j].reshape(gb, 2 * chunk, LANES)
                u3 = u3[:, 0:chunk] + u3[:, chunk:]
                for s in range(gb):
                    ds = _dot_tn(jnp.concatenate([v3[u, j][s], -u3[s]], axis=0),
                                 jnp.concatenate([k3[u, j][s], b3[u, j][s]], axis=0))
                    sbd_ref[bis[s], j] = p_end[u][s][:, lanes[j]] * (st[j][s] + jnp.where(m_pair, ds, 0.0))
        return carry

    lax.fori_loop(0, n_chunks // per_trip, row_chunks, 0)

    y = y_ref[...]
    inv_n = 1.0 / RW_HEAD
    d = y - head_bcast(head_sum(y) * inv_n)
    rstd = lax.rsqrt(head_sum(d * d) * inv_n + GN_EPS)
    gn = d * head_bcast(rstd) * gng_ref[...] + gnb_ref[...]
    bonus = head_bcast(head_sum(r_ref[...] * km_ref[...] * rk_ref[...])) * v_ref[...]
    o_ref[...] = ((gn + bonus) * gg).astype(o_ref.dtype)

    @pl.when(ti == pl.num_programs(1) - 1)
    def _finish():
        def unpack(bi, carry):
            for j in range(pairs):
                ns_ref[bi, 2 * j] = sbd_ref[bi, j, 0:RW_HEAD, 0:RW_HEAD]
                ns_ref[bi, 2 * j + 1] = pltpu.roll(sbd_ref[bi, j, RW_HEAD:ROWS, :], RW_HEAD, 1)[:, 0:RW_HEAD]
            return carry

        lax.fori_loop(0, bb, unpack, 0)


def _rwkv_branch(p_all, rw_col, sbuf, s0, ns_all, layer, P, *, t, bb, tt, chunk, name):
    b = p_all.shape[0] // t
    nt = t // tt
    nrw = sbuf.shape[-1]
    heads = s0.shape[2]
    dr = heads * RW_HEAD
    nl = nrw - 3 * dr
    rows = bb * tt
    xw = rw_col
    assert xw % LANES == 0 and xw < nrw <= 2 * xw and rw_col + 2 * xw <= p_all.shape[-1]
    assert dr % LANES == 0 and RW_HEAD % chunk == 0 and tt % chunk == 0 and rows % RW_HEAD == 0
    assert tt == chunk or bb == 1
    lw = lambda shape: pl.BlockSpec((None,) + shape, lambda i, j: (layer,) + (0,) * len(shape))
    kern = functools.partial(_rwkv_kernel, bb=bb, tt=tt, chunk=chunk, dr=dr)
    big = lambda: pltpu.VMEM((rows, dr), F32)
    args = (p_all, p_all, sbuf, s0, P["rw_mu_p"], P["rw_w0"], P["rw_w2_p"], P["rw_a0"], P["rw_a2_p"], P["rw_g2_p"],
            P["rw_kk"], P["rw_ka"], P["rw_rk"], P["rw_gn_g"], P["rw_gn_b"], P["seg"], P["segt"], ns_all)
    return pl.pallas_call(
        kern,
        grid=(b // bb, t // tt),
        in_specs=[pl.BlockSpec((rows, xw), lambda i, j: (i * nt + j, 1)),
                  pl.BlockSpec((rows, xw), lambda i, j: (i * nt + j, 2)),
                  pl.BlockSpec((None, bb, 1, nrw), lambda i, j: (layer, i, 0, 0)),
                  pl.BlockSpec((None, bb, heads, RW_HEAD, RW_HEAD), lambda i, j: (layer, i, 0, 0, 0)),
                  lw((1, nrw)), lw((1, dr)), lw((nl, dr)), lw((1, dr)), lw((nl, dr)), lw((nl, dr)),
                  lw((1, dr)), lw((1, dr)), lw((1, dr)), lw((1, dr)), lw((1, dr)),
                  pl.BlockSpec((dr, LANES), lambda i, j: (0, 0)),
                  pl.BlockSpec((LANES, dr), lambda i, j: (0, 0)),
                  pl.BlockSpec(memory_space=pl.ANY)],
        out_specs=[pl.BlockSpec((rows, dr), lambda i, j: (i * (t // tt) + j, 0)),
                   pl.BlockSpec((bb, 1, nrw), lambda i, j: (i, 0, 0)),
                   pl.BlockSpec((None, bb, heads, RW_HEAD, RW_HEAD), lambda i, j: (layer, i, 0, 0, 0))],
        out_shape=[jax.ShapeDtypeStruct((b * t, dr), BF16),
                   jax.ShapeDtypeStruct((b, 1, nrw), F32),
                   jax.ShapeDtypeStruct(ns_all.shape, F32)],
        input_output_aliases={len(args) - 1: 2},
        scratch_shapes=[pltpu.VMEM((bb, dr // LANES, ROWS, LANES), F32),
                        pltpu.VMEM((bb, 1, nrw), F32),
                        big(), big(), big(), big(), big(), big(), big(),
                        pltpu.VMEM((RW_HEAD, LANES), F32)],
        compiler_params=pltpu.CompilerParams(
            dimension_semantics=("arbitrary", "arbitrary"), vmem_limit_bytes=VMEM_LIMIT),
        name=name,
    )(*args)


def _trunk(x, mod, boff, states, P, cfg, tag):
    bsz, t_len, d = x.shape
    m = bsz * t_len
    depth = mod.shape[0]
    lru_conv, lru_h, rw_shift, rw_s, ffn_conv = states
    dl = lru_h.shape[-1]
    nrw = P["rw_mu_p"].shape[-1]
    nrw0 = rw_shift.shape[-1]
    outs = ([], [], [], [])
    tm, tn = cfg["tm"], cfg["tn"]
    x = x.reshape(m, d)
    ns_all = jnp.zeros(rw_s.shape, F32)
    lru_h4 = lru_h[:, :, None, :]
    sbuf = jnp.pad(rw_shift, ((0, 0), (0, 0), (0, nrw - nrw0)))[:, :, None, :]
    norm = functools.partial(_norm_mod, bb=cfg["nbb"], tt=cfg["ntt"])
    for l in range(depth):
        h = norm(x.reshape(bsz, t_len, d), P["norm_mix"], l, mod, 1, 0, boff, name=f"norm_mix_{tag}")
        in_proj = functools.partial(_in_proj, h, P["w_in_t"], l, n_lru=2 * dl, n_rw=nrw, n_gates=2 * d, tm=tm, tn=tn)
        p_all = in_proj(gates=False, name=f"in_proj_{tag}")
        p_gates = in_proj(gates=True, name=f"in_gates_{tag}")
        ga, n_lru_buf, n_lru_h = _lru_branch(p_all, lru_conv, lru_h4, l, P, t=t_len,
                                             bb=cfg["lbb"], tt=cfg["ltt"], name=f"lru_{tag}")
        o_rw, n_shift, ns_all = _rwkv_branch(p_all, 2 * dl, sbuf, rw_s, ns_all, l, P, t=t_len, bb=cfg["rbb"],
                                             tt=cfg["rtt"], chunk=cfg["chunk"], name=f"rwkv_{tag}")
        merged = _merge(ga, o_rw, P["w_pa"], P["w_pb"], p_gates, l, tm=tm, tn=cfg["tn_res"], name=f"merge_{tag}")
        x = _matmul_resid(merged, P["w_o"], l, x, mod, 2, boff, t_len, tm=tm, tn=tn, name=f"o_{tag}")

        h2 = norm(x.reshape(bsz, t_len, d), P["norm_ffn"], l, mod, 4, 3, boff, name=f"norm_ffn_{tag}")
        act, n_buf_g, n_buf_v = _ffn_up(h2, P["w_up"], P["ffn_conv_w"], P["ffn_conv_b"], ffn_conv, l,
                                        t_len=t_len, tm=tm, tn=cfg["tn_up"], name=f"ffn_up_{tag}")
        x = _matmul_resid(act, P["w_down"], l, x, mod, 5, boff, t_len, tm=cfg["tm_down"], tn=cfg["tn_res"],
                          name=f"ffn_down_{tag}")

        n_ffn_buf = jnp.concatenate([n_buf_g, n_buf_v], axis=-1)
        for lst, ns in zip(outs, (n_lru_buf, n_lru_h[:, 0, :], n_shift[:, 0, :nrw0], n_ffn_buf)):
            lst.append(ns)
    y = _final_norm(x.reshape(bsz, t_len, d), P["norm_final"], bb=cfg["nbb"], tt=cfg["ntt"], name=f"norm_final_{tag}")
    st = [jnp.stack(lst, axis=0) for lst in outs]
    return y, (st[0], st[1], st[2], ns_all, st[3])


def kernel(x_prompt, x_sample, c_prompt, c_sample, state_lru_conv, state_lru_h, state_rwkv_shift,
           state_rwkv_S, state_ffn_conv, w_ada, b_ada, norm_mix, norm_ffn, w_in, lru_conv_w,
           lru_conv_b, lru_wa, lru_ba, lru_wi, lru_bi, lru_lambda, w_pa, rw_mu, rw_w0, rw_w2, rw_a0,
           rw_a2, rw_g2, rw_kk, rw_ka, rw_rk, rw_gn_g, rw_gn_b, w_pb, w_o, w_up, ffn_conv_w,
           ffn_conv_b, w_down, norm_final):
    depth, d, _ = w_ada.shape
    bp, tp, _ = x_prompt.shape
    bs, ts, _ = x_sample.shape
    dl = lru_lambda.shape[-1]
    nblk, blk = lru_wa.shape[1], lru_wa.shape[2]
    heads = rw_rk.shape[1]
    dr = heads * RW_HEAD
    nrw0 = rw_mu.shape[-1]
    nl = _cdiv(nrw0 - 3 * dr, LANES) * LANES
    nrw = 3 * dr + nl
    lw_n, la_n = rw_w2.shape[1], rw_a2.shape[1]

    row = lambda p: p[:, None, :]
    per = LANES // blk
    eye = jnp.eye(per, dtype=F32)
    block_diag = lambda w: (eye[:, None, :, None] * w.reshape(depth, nblk // per, per, blk, 1, blk)
                            ).reshape(depth, nblk // per, LANES, LANES).astype(BF16)
    pad_rows = lambda w, off: jnp.pad(w, ((0, 0), (off, nl - off - w.shape[1]), (0, 0))).astype(BF16)
    head_of = jnp.arange(dr) // RW_HEAD
    seg = (head_of[:, None] == jnp.arange(LANES)[None, :]).astype(BF16)
    P = dict(
        norm_mix=row(norm_mix), norm_ffn=row(norm_ffn), norm_final=norm_final[None, :],
        w_in_t=jnp.swapaxes(w_in, 1, 2), lru_conv_w=lru_conv_w, lru_conv_b=row(lru_conv_b), lru_wa_bd=block_diag(lru_wa),
        lru_ba=row(lru_ba), lru_wi_bd=block_diag(lru_wi), lru_bi=row(lru_bi), lru_lambda=row(lru_lambda),
        w_pa=w_pa, w_pb=w_pb, w_o=w_o, w_up=w_up, w_down=w_down,
        rw_mu_p=row(jnp.pad(rw_mu, ((0, 0), (0, nrw - nrw0)))),
        rw_w0=row(rw_w0), rw_a0=row(rw_a0),
        rw_w2_p=pad_rows(rw_w2, 0), rw_a2_p=pad_rows(rw_a2, lw_n), rw_g2_p=pad_rows(rw_g2, lw_n + la_n),
        rw_kk=row(rw_kk), rw_ka=row(rw_ka), rw_rk=rw_rk.reshape(depth, 1, dr),
        rw_gn_g=row(rw_gn_g), rw_gn_b=row(rw_gn_b), seg=seg, segt=seg.T,
        ffn_conv_w=ffn_conv_w, ffn_conv_b=row(ffn_conv_b),
    )

    nb = bp + bs
    nb_pad = _cdiv(nb, SUBLANES) * SUBLANES
    c_all = jnp.concatenate([c_sample, c_prompt, jnp.zeros((nb_pad - nb, d), F32)], axis=0)
    mod = _adaln(c_all, w_ada, row(b_ada), tn=1024)[:, :, None, :]

    zeros = lambda *s: jnp.zeros((depth, bp) + s, F32)
    p_states = (zeros(lru_conv_w.shape[1] - 1, dl), zeros(dl), zeros(nrw0),
                zeros(heads, RW_HEAD, RW_HEAD), zeros(ffn_conv_w.shape[1] - 1, w_up.shape[-1]))
    s_states = (state_lru_conv, state_lru_h, state_rwkv_shift, state_rwkv_S, state_ffn_conv)

    tiles = dict(tm=1024, tn=1024, tn_res=512, tn_up=512, tm_down=512)
    cfg_p = dict(tiles, nbb=1, ntt=min(tp, 512), lbb=1, ltt=min(tp, 512),
                 rbb=1, rtt=min(tp, 256), chunk=min(tp, RW_HEAD))
    sb = min(bs, RW_HEAD // ts) if ts == SUBLANES else 1
    cfg_s = dict(tiles, nbb=min(bs, 64), ntt=ts, lbb=min(bs, 32), ltt=ts, rbb=sb, rtt=ts, chunk=ts)
    y_p, st_p = _trunk(x_prompt, mod, bs, p_states, P, cfg_p, "prompt")
    y_s, st_s = _trunk(x_sample, mod, 0, s_states, P, cfg_s, "sample")
    return (y_p, y_s) + st_p + st_s
```

```python
import functools
import math

import jax
import jax.numpy as jnp
from jax import lax
from jax.experimental import pallas as pl
from jax.experimental.pallas import tpu as pltpu

F32 = jnp.float32
BF16 = jnp.bfloat16

LANES = 128
SUBLANES = 8
RW_HEAD = 64
ROWS = 2 * RW_HEAD
LRU_C = 8.0
RMS_EPS = 1e-6
GN_EPS = 64e-5
VMEM_LIMIT = 56 * 1024 * 1024


def _cdiv(a, b):
    return -(-a // b)


def _dot(a, b):
    return jnp.dot(a.astype(BF16), b.astype(BF16), preferred_element_type=F32)


def _dot_nt(a, b):
    return lax.dot_general(a.astype(BF16), b.astype(BF16), (((1,), (1,)), ((), ())),
                           preferred_element_type=F32)


def _dot_tn(a, b):
    return lax.dot_general(a.astype(BF16), b.astype(BF16), (((0,), (0,)), ((), ())),
                           preferred_element_type=F32)


def _dot_hi(a, b):
    n = b.shape[1]
    a_hi, a_lo = _split(a, 2)
    b_hi, b_lo = _split(b, 2)
    lhs = jnp.concatenate([a_hi, a_lo], axis=1)
    rhs = jnp.concatenate([jnp.concatenate([b_hi, b_lo], axis=1),
                           jnp.concatenate([b_hi, jnp.zeros_like(b_lo)], axis=1)], axis=0)
    out = jnp.dot(lhs, rhs, preferred_element_type=F32)
    return out[:, :n] + out[:, n:]


def _split(x, n):
    parts = []
    for _ in range(n - 1):
        p = x.astype(BF16)
        parts.append(p)
        x = x - p.astype(F32)
    parts.append(x.astype(BF16))
    return parts


def _dot_exact_rhs(x, m, n=3):
    return sum(jnp.dot(p, m, preferred_element_type=F32) for p in _split(x, n))


def _dot_exact_lhs(m, x, n=3):
    return sum(jnp.dot(m, p, preferred_element_type=F32) for p in _split(x, n))


def _sigmoid(x):
    return 0.5 * jnp.tanh(0.5 * x) + 0.5


def _softplus(x):
    return jnp.maximum(x, 0.0) + jnp.log1p(jnp.exp(-jnp.abs(x)))


def _silu(x):
    return x * _sigmoid(x)


def _gelu_tanh(x):
    return 0.5 * x * (1.0 + jnp.tanh(math.sqrt(2.0 / math.pi) * (x + 0.044715 * (x * x * x))))


def _mm_resid_kernel(a_ref, w_ref, x_ref, gt_ref, o_ref, wbf_ref):
    @pl.when(pl.program_id(1) == 0)
    def _cast_weights():
        wbf_ref[...] = w_ref[...].astype(BF16)

    acc = jnp.dot(a_ref[...], wbf_ref[...], preferred_element_type=F32)
    nseq = gt_ref.shape[0]
    tm, tn = acc.shape
    out = x_ref[...].reshape(nseq, tm // nseq, tn) + gt_ref[...] * acc.reshape(nseq, tm // nseq, tn)
    o_ref[...] = out.reshape(tm, tn)


def _pick_tile(n, target):
    assert n % LANES == 0
    units = n // LANES
    best = max(u for u in range(1, units + 1) if units % u == 0 and u * LANES <= max(target, LANES))
    return best * LANES


def _gate_spec(layer, gate_col, boff, t_len, tm, tn, d):
    col = lambda j: gate_col * (d // tn) + j
    if tm <= t_len:
        assert t_len % tm == 0
        nt = t_len // tm
        return pl.BlockSpec((None, 1, 1, tn), lambda j, i: (layer, boff + i // nt, 0, col(j)))
    assert tm % t_len == 0 and boff % (tm // t_len) == 0
    nseq = tm // t_len
    return pl.BlockSpec((None, nseq, 1, tn), lambda j, i: (layer, boff // nseq + i, 0, col(j)))


def _matmul_resid(a, w, layer, x, mod, gate_col, boff, t_len, *, tm, tn, name):
    m, k = a.shape
    n = w.shape[-1]
    tm = min(tm, m)
    tn = _pick_tile(n, tn)
    assert m % tm == 0
    return pl.pallas_call(
        _mm_resid_kernel,
        grid=(n // tn, m // tm),
        in_specs=[pl.BlockSpec((tm, k), lambda j, i: (i, 0)),
                  pl.BlockSpec((None, k, tn), lambda j, i: (layer, 0, j)),
                  pl.BlockSpec((tm, tn), lambda j, i: (i, j)),
                  _gate_spec(layer, gate_col, boff, t_len, tm, tn, n)],
        out_specs=pl.BlockSpec((tm, tn), lambda j, i: (i, j)),
        out_shape=jax.ShapeDtypeStruct((m, n), F32),
        scratch_shapes=[pltpu.VMEM((k, tn), BF16)],
        compiler_params=pltpu.CompilerParams(
            dimension_semantics=("arbitrary", "arbitrary"), vmem_limit_bytes=VMEM_LIMIT),
        name=name,
    )(a, w, x, mod)


def _in_proj_kernel(a_ref, wa_ref, wb_ref, o_ref, wbf_ref, *, j0, n_plain, shift):
    j = pl.program_id(0) + j0
    first_row_tile = pl.program_id(1) == 0
    tn = wa_ref.shape[0]

    @pl.when(first_row_tile & (j < n_plain))
    def _cast_weights():
        wbf_ref[...] = wa_ref[...].astype(BF16)

    @pl.when(first_row_tile & (j >= n_plain))
    def _cast_shifted_weights():
        wbf_ref[0:tn - shift, :] = wa_ref[shift:tn, :].astype(BF16)
        if shift:
            wbf_ref[tn - shift:tn, :] = wb_ref[0:shift, :].astype(BF16)

    o_ref[...] = lax.dot_general(a_ref[...], wbf_ref[...], (((1,), (1,)), ((), ())),
                                 preferred_element_type=F32).astype(o_ref.dtype)


def _in_proj(a, w_in_t, layer, *, n_lru, n_rw, n_gates, gates, tm, tn, name):
    m, k = a.shape
    n_in = w_in_t.shape[1]
    tm = min(tm, m)
    tn = _pick_tile(math.gcd(n_lru, n_gates), tn)
    n_rw_t = _cdiv(n_rw, tn) * tn
    gates_start = n_in - n_gates
    n_plain = (n_lru + n_rw_t) // tn
    base, shift = gates_start // tn, gates_start % tn
    bf16_rows = 2 * SUBLANES
    assert m % tm == 0 and n_lru + n_rw_t <= n_in and shift % bf16_rows == 0
    wbw = tn // 2 if (tn // 2) % LANES == 0 and shift <= tn // 2 else tn
    j0, nj = (n_plain, n_gates // tn) if gates else (0, n_plain)
    wa_idx = lambda j: jnp.where(j + j0 < n_plain, j + j0, j + j0 - n_plain + base)
    wb_idx = lambda j: (jnp.maximum(j + j0 - n_plain, 0) + base + (1 if shift else 0)) * (tn // wbw)
    return pl.pallas_call(
        functools.partial(_in_proj_kernel, j0=j0, n_plain=n_plain, shift=shift),
        grid=(nj, m // tm),
        in_specs=[pl.BlockSpec((tm, k), lambda j, i: (i, 0)),
                  pl.BlockSpec((None, tn, k), lambda j, i: (layer, wa_idx(j), 0)),
                  pl.BlockSpec((None, wbw, k), lambda j, i: (layer, wb_idx(j), 0))],
        out_specs=pl.BlockSpec((tm, tn), lambda j, i: (i, j)),
        out_shape=jax.ShapeDtypeStruct((m, nj * tn), BF16 if gates else F32),
        scratch_shapes=[pltpu.VMEM((tn, k), BF16)],
        compiler_params=pltpu.CompilerParams(
            dimension_semantics=("arbitrary", "arbitrary"), vmem_limit_bytes=VMEM_LIMIT),
        name=name,
    )(a, w_in_t, w_in_t)


def _merge_kernel(ga_ref, orw_ref, wpa_ref, wpb_ref, sa_ref, sb_ref, o_ref, wa_bf_ref, wb_bf_ref):
    @pl.when(pl.program_id(1) == 0)
    def _cast_weights():
        wa_bf_ref[...] = wpa_ref[...].astype(BF16)
        wb_bf_ref[...] = wpb_ref[...].astype(BF16)

    y_a = jnp.dot(ga_ref[...], wa_bf_ref[...], preferred_element_type=F32)
    y_b = jnp.dot(orw_ref[...], wb_bf_ref[...], preferred_element_type=F32)
    s_a = _sigmoid(sa_ref[...].astype(F32))
    s_b = _sigmoid(sb_ref[...].astype(F32))
    o_ref[...] = (s_a * y_a + s_b * y_b).astype(o_ref.dtype)


def _merge(ga, o_rw, w_pa, w_pb, gates, layer, *, tm, tn, name):
    m, ca = ga.shape
    cb = o_rw.shape[1]
    d = w_pa.shape[-1]
    tm = min(tm, m)
    tn = _pick_tile(d, tn)
    nd = d // tn
    assert m % tm == 0 and gates.shape == (m, 2 * d)
    return pl.pallas_call(
        _merge_kernel,
        grid=(nd, m // tm),
        in_specs=[pl.BlockSpec((tm, ca), lambda j, i: (i, 0)),
                  pl.BlockSpec((tm, cb), lambda j, i: (i, 0)),
                  pl.BlockSpec((None, ca, tn), lambda j, i: (layer, 0, j)),
                  pl.BlockSpec((None, cb, tn), lambda j, i: (layer, 0, j)),
                  pl.BlockSpec((tm, tn), lambda j, i: (i, j)),
                  pl.BlockSpec((tm, tn), lambda j, i: (i, nd + j))],
        out_specs=pl.BlockSpec((tm, tn), lambda j, i: (i, j)),
        out_shape=jax.ShapeDtypeStruct((m, d), BF16),
        scratch_shapes=[pltpu.VMEM((ca, tn), BF16), pltpu.VMEM((cb, tn), BF16)],
        compiler_params=pltpu.CompilerParams(
            dimension_semantics=("arbitrary", "arbitrary"), vmem_limit_bytes=VMEM_LIMIT),
        name=name,
    )(ga, o_rw, w_pa, w_pb, gates, gates)


def _ffn_up_kernel(a_ref, wg_ref, wv_ref, cwg_ref, cwv_ref, cbg_ref, cbv_ref, bg_ref, bv_ref,
                   o_ref, nbg_ref, nbv_ref, wgb_ref, wvb_ref, xg_ref, xv_ref, *, bb, tt, nt, width):
    i = pl.program_id(1)
    tn = o_ref.shape[-1]
    base = SUBLANES - (width - 1)

    @pl.when(i == 0)
    def _cast_weights():
        wgb_ref[...] = wg_ref[...].astype(BF16)
        wvb_ref[...] = wv_ref[...].astype(BF16)

    @pl.when(i % nt == 0)
    def _sequence_start():
        xg_ref[:, base:SUBLANES, :] = bg_ref[...]
        xv_ref[:, base:SUBLANES, :] = bv_ref[...]

    a = a_ref[...]

    def half(wb_ref, x_ref, cw_ref, cb_ref, nb_ref):
        x_ref[:, SUBLANES:SUBLANES + tt, :] = jnp.dot(
            a, wb_ref[...], preferred_element_type=F32).reshape(bb, tt, tn)
        conv = cb_ref[...] + cw_ref[0:1, :] * x_ref[:, base:base + tt, :]
        for j in range(1, width):
            conv = conv + cw_ref[j:j + 1, :] * x_ref[:, base + j:base + j + tt, :]
        last = x_ref[:, tt + base:tt + SUBLANES, :]
        nb_ref[...] = last
        x_ref[:, base:SUBLANES, :] = last
        return conv

    cg = half(wgb_ref, xg_ref, cwg_ref, cbg_ref, nbg_ref)
    cv = half(wvb_ref, xv_ref, cwv_ref, cbv_ref, nbv_ref)
    o_ref[...] = (_silu(cg) * cv).reshape(bb * tt, tn).astype(o_ref.dtype)


def _ffn_up(a, w, cw, cb, buf, layer, *, t_len, tm, tn, name):
    m, d = a.shape
    f = w.shape[-1] // 2
    width = cw.shape[1]
    bsz = m // t_len
    tm = min(tm, m)
    tn = _pick_tile(f, tn)
    nf = f // tn
    if tm <= t_len:
        assert t_len % tm == 0
        bb, tt = 1, tm
    else:
        assert tm % t_len == 0 and t_len == SUBLANES
        bb, tt = tm // t_len, t_len
    nt = t_len // tt
    assert t_len >= width - 1
    wspec = lambda off: pl.BlockSpec((None, d, tn), lambda j, i: (layer, 0, j + off))
    cspec = lambda rows, off: pl.BlockSpec((None, rows, tn), lambda j, i: (layer, 0, j + off))
    bspec = lambda off: pl.BlockSpec((None, bb, width - 1, tn), lambda j, i: (layer, i // nt, 0, j + off))
    ospec = pl.BlockSpec((bb, width - 1, tn), lambda j, i: (i // nt, 0, j))
    return pl.pallas_call(
        functools.partial(_ffn_up_kernel, bb=bb, tt=tt, nt=nt, width=width),
        grid=(nf, m // tm),
        in_specs=[pl.BlockSpec((tm, d), lambda j, i: (i, 0)), wspec(0), wspec(nf),
                  cspec(width, 0), cspec(width, nf), cspec(1, 0), cspec(1, nf), bspec(0), bspec(nf)],
        out_specs=[pl.BlockSpec((tm, tn), lambda j, i: (i, j)), ospec, ospec],
        out_shape=[jax.ShapeDtypeStruct((m, f), BF16),
                   jax.ShapeDtypeStruct((bsz, width - 1, f), F32),
                   jax.ShapeDtypeStruct((bsz, width - 1, f), F32)],
        scratch_shapes=[pltpu.VMEM((d, tn), BF16), pltpu.VMEM((d, tn), BF16),
                        pltpu.VMEM((bb, tt + SUBLANES, tn), F32), pltpu.VMEM((bb, tt + SUBLANES, tn), F32)],
        compiler_params=pltpu.CompilerParams(
            dimension_semantics=("arbitrary", "arbitrary"), vmem_limit_bytes=VMEM_LIMIT),
        name=name,
    )(a, w, w, cw, cw, cb, cb, buf, buf)


def _norm_mod_kernel(x_ref, g_ref, sc_ref, sh_ref, o_ref):
    x = x_ref[...]
    y = x * lax.rsqrt(jnp.mean(x * x, axis=-1, keepdims=True) + RMS_EPS) * g_ref[...]
    y = y * (1.0 + sc_ref[...]) + sh_ref[...]
    o_ref[...] = y.reshape(o_ref.shape).astype(o_ref.dtype)


def _adaln_kernel(c_ref, w_ref, b_ref, o_ref):
    o_ref[...] = jnp.dot(_silu(c_ref[...]).astype(BF16), w_ref[...].astype(BF16),
                         preferred_element_type=F32) + b_ref[...]


def _adaln(c_all, w_ada, b_ada, *, tn):
    nb, d = c_all.shape
    depth, _, n = w_ada.shape
    tn = _pick_tile(n, tn)
    return pl.pallas_call(
        _adaln_kernel,
        grid=(depth, n // tn),
        in_specs=[pl.BlockSpec((nb, d), lambda l, j: (0, 0)),
                  pl.BlockSpec((None, d, tn), lambda l, j: (l, 0, j)),
                  pl.BlockSpec((None, 1, tn), lambda l, j: (l, 0, j))],
        out_specs=pl.BlockSpec((None, nb, tn), lambda l, j: (l, 0, j)),
        out_shape=jax.ShapeDtypeStruct((depth, nb, n), F32),
        compiler_params=pltpu.CompilerParams(
            dimension_semantics=("arbitrary", "arbitrary"), vmem_limit_bytes=VMEM_LIMIT),
        name="adaln",
    )(c_all, w_ada, b_ada)


def _norm_mod(x, g, layer, mod, sc_idx, sh_idx, boff, *, bb, tt, name):
    b, t, d = x.shape
    nt = t // tt
    assert boff % bb == 0
    return pl.pallas_call(
        _norm_mod_kernel,
        grid=(b // bb, nt),
        in_specs=[pl.BlockSpec((bb, tt, d), lambda i, j: (i, j, 0)),
                  pl.BlockSpec((None, 1, d), lambda i, j: (layer, 0, 0)),
                  pl.BlockSpec((None, bb, 1, d), lambda i, j: (layer, boff // bb + i, 0, sc_idx)),
                  pl.BlockSpec((None, bb, 1, d), lambda i, j: (layer, boff // bb + i, 0, sh_idx))],
        out_specs=pl.BlockSpec((bb * tt, d), lambda i, j: (i * nt + j, 0)),
        out_shape=jax.ShapeDtypeStruct((b * t, d), BF16),
        compiler_params=pltpu.CompilerParams(
            dimension_semantics=("arbitrary", "arbitrary"), vmem_limit_bytes=VMEM_LIMIT),
        name=name,
    )(x, g, mod, mod)


def _final_norm_kernel(x_ref, g_ref, o_ref):
    x = x_ref[...]
    o_ref[...] = x * lax.rsqrt(jnp.mean(x * x, axis=-1, keepdims=True) + RMS_EPS) * g_ref[...]


def _final_norm(x, g, *, bb, tt, name):
    b, t, d = x.shape
    return pl.pallas_call(
        _final_norm_kernel,
        grid=(b // bb, t // tt),
        in_specs=[pl.BlockSpec((bb, tt, d), lambda i, j: (i, j, 0)),
                  pl.BlockSpec((1, d), lambda i, j: (0, 0))],
        out_specs=pl.BlockSpec((bb, tt, d), lambda i, j: (i, j, 0)),
        out_shape=jax.ShapeDtypeStruct((b, t, d), F32),
        compiler_params=pltpu.CompilerParams(
            dimension_semantics=("arbitrary", "arbitrary"), vmem_limit_bytes=VMEM_LIMIT),
        name=name,
    )(x, g)


def _lru_kernel(x_ref, gate_ref, buf_ref, h0_ref, cw_ref, cb_ref, wa_ref, ba_ref, wi_ref, bi_ref,
                lam_ref, ga_ref, nbuf_ref, nh_ref, xx_ref, hc_ref, a_ref, b_ref, h_ref, *, bb, tt, width):
    c = x_ref.shape[-1]
    rows = bb * tt
    halo = width - 1
    base = SUBLANES - halo

    @pl.when(pl.program_id(1) == 0)
    def _init():
        xx_ref[:, base:SUBLANES, :] = buf_ref[...]
        hc_ref[...] = h0_ref[...]

    xx_ref[:, SUBLANES:SUBLANES + tt, :] = x_ref[...].reshape(bb, tt, c)
    conv = cb_ref[...] + cw_ref[0:1, :] * xx_ref[:, base:base + tt, :]
    for j in range(1, width):
        conv = conv + cw_ref[j:j + 1, :] * xx_ref[:, base + j:base + j + tt, :]
    last = xx_ref[:, tt + base:tt + SUBLANES, :]
    nbuf_ref[...] = last
    xx_ref[:, base:SUBLANES, :] = last

    xc = conv.reshape(rows, c)
    xcb = xc.astype(BF16)

    def gate(w_ref, b_ref):
        pre = [jnp.dot(xcb[:, g * LANES:(g + 1) * LANES], w_ref[g], preferred_element_type=F32)
               for g in range(c // LANES)]
        return _sigmoid(jnp.concatenate(pre, axis=1) + b_ref[...])

    r = gate(wa_ref, ba_ref)
    i = gate(wi_ref, bi_ref)
    log_a = -LRU_C * r * _softplus(-lam_ref[...])
    a = jnp.exp(log_a)
    inp = jnp.sqrt(-jnp.tanh(log_a) * (a * a + 1.0)) * (i * xc)

    pos = lax.broadcasted_iota(jnp.int32, (rows, 1), 0) % SUBLANES
    for s in (1, 2, 4):
        a_sh = pltpu.roll(a, s, 0)
        b_sh = pltpu.roll(inp, s, 0)
        m = pos >= s
        inp = jnp.where(m, a * b_sh + inp, inp)
        a = jnp.where(m, a * a_sh, a)

    if tt == SUBLANES:
        h = (inp.reshape(bb, tt, c) + a.reshape(bb, tt, c) * hc_ref[...]).reshape(rows, c)
    else:
        a_ref[...] = a
        b_ref[...] = inp

        def group(g, carry):
            sl = pl.ds(pl.multiple_of(g * SUBLANES, SUBLANES), SUBLANES)
            hg = b_ref[sl, :] + a_ref[sl, :] * carry
            h_ref[sl, :] = hg
            return hg[SUBLANES - 1:SUBLANES, :]

        lax.fori_loop(0, rows // SUBLANES, group, hc_ref[0])
        h = h_ref[...]

    h3 = h.reshape(bb, tt, c)
    hc_ref[...] = h3[:, tt - 1:tt, :]
    nh_ref[...] = h3[:, tt - 1:tt, :]
    ga_ref[...] = (_gelu_tanh(gate_ref[...]) * h).astype(ga_ref.dtype)


def _lru_branch(p_lru, buf, h0, layer, P, *, t, bb, tt, name):
    b = p_lru.shape[0] // t
    c = h0.shape[-1]
    width = P["lru_conv_w"].shape[1]
    nt = t // tt
    assert t >= width - 1 and (tt == SUBLANES or bb == 1)
    lw = lambda shape: pl.BlockSpec((None,) + shape, lambda i, j: (layer,) + (0,) * len(shape))
    kern = functools.partial(_lru_kernel, bb=bb, tt=tt, width=width)
    return pl.pallas_call(
        kern,
        grid=(b // bb, nt),
        in_specs=[pl.BlockSpec((bb * tt, c), lambda i, j: (i * nt + j, 0)),
                  pl.BlockSpec((bb * tt, c), lambda i, j: (i * nt + j, 1)),
                  pl.BlockSpec((None, bb, width - 1, c), lambda i, j: (layer, i, 0, 0)),
                  pl.BlockSpec((None, bb, 1, c), lambda i, j: (layer, i, 0, 0)),
                  lw((width, c)), lw((1, c)), lw((c // LANES, LANES, LANES)), lw((1, c)),
                  lw((c // LANES, LANES, LANES)), lw((1, c)), lw((1, c))],
        out_specs=[pl.BlockSpec((bb * tt, c), lambda i, j: (i * (t // tt) + j, 0)),
                   pl.BlockSpec((bb, width - 1, c), lambda i, j: (i, 0, 0)),
                   pl.BlockSpec((bb, 1, c), lambda i, j: (i, 0, 0))],
        out_shape=[jax.ShapeDtypeStruct((b * t, c), BF16),
                   jax.ShapeDtypeStruct((b, width - 1, c), F32),
                   jax.ShapeDtypeStruct((b, 1, c), F32)],
        scratch_shapes=[pltpu.VMEM((bb, tt + SUBLANES, c), F32),
                        pltpu.VMEM((bb, 1, c), F32),
                        pltpu.VMEM((bb * tt, c), F32),
                        pltpu.VMEM((bb * tt, c), F32),
                        pltpu.VMEM((bb * tt, c), F32)],
        compiler_params=pltpu.CompilerParams(
            dimension_semantics=("arbitrary", "arbitrary"), vmem_limit_bytes=VMEM_LIMIT),
        name=name,
    )(p_lru, p_lru, buf, h0, P["lru_conv_w"], P["lru_conv_b"], P["lru_wa_bd"], P["lru_ba"],
      P["lru_wi_bd"], P["lru_bi"], P["lru_lambda"])


def _rwkv_kernel(xa_ref, xb_ref, sbuf_ref, s0_ref, mu_ref, w0_ref, w2_ref, a0_ref, a2_ref, g2_ref, kkp_ref,
                 ka_ref, rk_ref, gng_ref, gnb_ref, seg_ref, segt_ref, ns_all_ref,
                 o_ref, nshift_ref, ns_ref,
                 sbd_ref, prev_ref, r_ref, kk_ref, km_ref, b_ref, v_ref, lw_ref, y_ref, tmp_ref,
                 *, bb, tt, chunk, dr):
    del ns_all_ref
    rows = bb * tt
    nrw = sbuf_ref.shape[-1]
    pairs = dr // LANES
    gb = RW_HEAD // chunk
    ti = pl.program_id(1)

    @pl.when(ti == 0)
    def _init():
        prev_ref[...] = sbuf_ref[...]
        tmp_ref[...] = jnp.zeros_like(tmp_ref)

        def pack(bi, carry):
            for j in range(pairs):
                sbd_ref[bi, j, 0:RW_HEAD, 0:RW_HEAD] = s0_ref[bi, 2 * j]
                sbd_ref[bi, j, 0:RW_HEAD, RW_HEAD:LANES] = jnp.zeros((RW_HEAD, RW_HEAD), F32)
                tmp_ref[:, 0:RW_HEAD] = s0_ref[bi, 2 * j + 1]
                sbd_ref[bi, j, RW_HEAD:ROWS, :] = pltpu.roll(tmp_ref[...], RW_HEAD, 1)
            return carry

        lax.fori_loop(0, bb, pack, 0)

    x2 = jnp.concatenate([xa_ref[...], xb_ref[:, 0:nrw - xa_ref.shape[-1]]], axis=-1)
    x3 = x2.reshape(bb, tt, nrw)
    rolled = pltpu.roll(x2, 1, 0).reshape(bb, tt, nrw)
    t_pos = lax.broadcasted_iota(jnp.int32, (bb, tt, 1), 1)
    prev3 = jnp.where(t_pos == 0, prev_ref[...], rolled)
    last = x3[:, tt - 1:tt, :]
    prev_ref[...] = last
    nshift_ref[...] = last
    xs = (x3 + (prev3 - x3) * mu_ref[...]).reshape(rows, nrw)

    r = xs[:, 0:dr]
    k = xs[:, dr:2 * dr]
    v = xs[:, 2 * dr:3 * dr]
    lora = xs[:, 3 * dr:]
    log_decay = -math.exp(-0.5) * _sigmoid(w0_ref[...] + _dot(jnp.tanh(lora), w2_ref[...]))
    a = _sigmoid(a0_ref[...] + _dot(lora, a2_ref[...]))
    gg = _dot(_sigmoid(lora), g2_ref[...])

    seg = seg_ref[...]
    segt = segt_ref[...]
    head_sum = lambda z, n=2: _dot_exact_rhs(z, seg, n)
    head_bcast = lambda z: _dot_exact_rhs(z, segt, 2)

    kk = k * kkp_ref[...]
    kk = kk * head_bcast(1.0 / jnp.maximum(jnp.sqrt(head_sum(kk * kk)), 1e-12))
    km = k * (1.0 + (a - 1.0) * ka_ref[...])
    r_ref[...] = r
    kk_ref[...] = kk
    km_ref[...] = km
    b_ref[...] = kk * a
    v_ref[...] = v
    lw_ref[...] = log_decay

    ri = lax.broadcasted_iota(jnp.int32, (ROWS, ROWS), 0)
    ci = lax.broadcasted_iota(jnp.int32, (ROWS, ROWS), 1)
    same = (ri // chunk) == (ci // chunk)
    m_strict = same & ((ci % chunk) < (ri % chunk))
    m_incl = same & ((ci % chunk) <= (ri % chunk))
    m_pair = (ri // RW_HEAD) == (ci // RW_HEAD)
    eye = jnp.where(ri == ci, 1.0, 0.0)
    lvl_masks = []
    s = 1
    while s < chunk:
        lvl_masks.append(((ri // (2 * s)) == (ci // (2 * s))) & ((ri // s) != (ci // s)))
        s *= 2
    r64 = lax.broadcasted_iota(jnp.int32, (RW_HEAD, RW_HEAD), 0)
    c64 = lax.broadcasted_iota(jnp.int32, (RW_HEAD, RW_HEAD), 1)
    tril = jnp.where(((r64 // chunk) == (c64 // chunk)) & (c64 <= r64), 1.0, 0.0).astype(BF16)
    lane_lo = lax.broadcasted_iota(jnp.int32, (1, 1, LANES), 2) < RW_HEAD

    def stack_par(z3):
        return jnp.concatenate([jnp.where(lane_lo, z3, 0.0), jnp.where(lane_lo, 0.0, z3)],
                               axis=1).reshape(ROWS, LANES)

    def stack_dup(z3):
        return jnp.concatenate([z3, z3], axis=1).reshape(ROWS, LANES)

    pr = range(pairs)
    lanes = [slice(j * LANES, (j + 1) * LANES) for j in pr]
    n_chunks = rows // RW_HEAD
    per_trip = math.gcd(n_chunks, 4)
    prob = [(u, j) for u in range(per_trip) for j in pr]

    def row_chunks(it, carry):
        rcs = [it * per_trip + u for u in range(per_trip)]
        sls = [pl.ds(pl.multiple_of(rc * RW_HEAD, RW_HEAD), RW_HEAD) for rc in rcs]
        q3, r3, k3, b3, v3, p_end = {}, {}, {}, {}, {}, []
        for u, sl in enumerate(sls):
            lw = lw_ref[sl, :]
            c_in = _dot_exact_lhs(tril, lw, 3)
            p_in = jnp.exp(c_in)
            p_inv = jnp.exp(-c_in)
            slabs = (kk_ref[sl, :] * jnp.exp(c_in - lw), r_ref[sl, :] * p_in, km_ref[sl, :] * p_inv,
                     b_ref[sl, :] * p_inv, v_ref[sl, :])
            p_end.append(p_in.reshape(gb, chunk, dr)[:, chunk - 1:chunk, :])
            for j in pr:
                q3[u, j], r3[u, j], k3[u, j], b3[u, j], v3[u, j] = (
                    z[:, lanes[j]].reshape(gb, chunk, LANES) for z in slabs)
        lq = {p: stack_par(q3[p]) for p in prob}
        lr = {p: stack_par(r3[p]) for p in prob}
        vm = {p: stack_par(v3[p]) for p in prob}
        g = {p: _dot_nt(jnp.concatenate([lq[p], lr[p]], axis=0),
                        jnp.concatenate([stack_dup(b3[p]), stack_dup(k3[p])], axis=0)) for p in prob}
        l_b = {p: jnp.where(m_strict, g[p][0:ROWS, 0:ROWS], 0.0) for p in prob}
        n_kb = {p: jnp.concatenate([jnp.where(m_incl, g[p][ROWS:, ROWS:], 0.0),
                                    jnp.where(m_incl, -g[p][ROWS:, 0:ROWS], 0.0)], axis=1) for p in prob}
        mkv = {p: _dot(jnp.where(m_strict, g[p][0:ROWS, ROWS:], 0.0), vm[p]) for p in prob}
        t_inv = {p: eye - jnp.where(lvl_masks[0], l_b[p], 0.0) for p in prob}
        for lm in lvl_masks[1:]:
            w = {p: _dot(t_inv[p], jnp.where(lm, l_b[p], 0.0)) for p in prob}
            t_inv = {p: t_inv[p] - _dot(w[p], t_inv[p]) for p in prob}
        for u, (rc, sl) in enumerate(zip(rcs, sls)):
            bis = [0 if bb == 1 else (rc * RW_HEAD + s * chunk) // tt for s in range(gb)]
            st = [[sbd_ref[bis[s], j] for s in range(gb)] for j in pr]
            qa, ra = [], []
            for j in pr:
                lq3 = lq[u, j].reshape(gb, 2 * chunk, LANES)
                lr3 = lr[u, j].reshape(gb, 2 * chunk, LANES)
                qr = [_dot_nt(jnp.concatenate([lq3[s], lr3[s]], axis=0), st[j][s]) for s in range(gb)]
                qa.append(jnp.concatenate([z[0:2 * chunk] for z in qr], axis=0) if gb > 1 else qr[0][0:2 * chunk])
                ra.append(jnp.concatenate([z[2 * chunk:] for z in qr], axis=0) if gb > 1 else qr[0][2 * chunk:])
            x = [_dot_hi(t_inv[u, j], qa[j] + mkv[u, j]) for j in pr]
            ys = [ra[j] + _dot(n_kb[u, j], jnp.concatenate([vm[u, j], x[j]], axis=0)) for j in pr]
            for j in pr:
                ys3 = ys[j].reshape(gb, 2 * chunk, LANES)
                y_ref[sl, lanes[j]] = (ys3[:, 0:chunk] + ys3[:, chunk:]).reshape(RW_HEAD, LANES)
            for j in pr:
                u3 = x[j].reshape(gb, 2 * chunk, LANES)
                u3 = u3[:, 0:chunk] + u3[:, chunk:]
                for s in range(gb):
                    ds = _dot_tn(jnp.concatenate([v3[u, j][s], -u3[s]], axis=0),
                                 jnp.concatenate([k3[u, j][s], b3[u, j][s]], axis=0))
                    sbd_ref[bis[s], j] = p_end[u][s][:, lanes[j]] * (st[j][s] + jnp.where(m_pair, ds, 0.0))
        return carry

    lax.fori_loop(0, n_chunks // per_trip, row_chunks, 0)

    y = y_ref[...]
    inv_n = 1.0 / RW_HEAD
    d = y - head_bcast(head_sum(y) * inv_n)
    rstd = lax.rsqrt(head_sum(d * d) * inv_n + GN_EPS)
    gn = d * head_bcast(rstd) * gng_ref[...] + gnb_ref[...]
    bonus = head_bcast(head_sum(r_ref[...] * km_ref[...] * rk_ref[...])) * v_ref[...]
    o_ref[...] = ((gn + bonus) * gg).astype(o_ref.dtype)

    @pl.when(ti == pl.num_programs(1) - 1)
    def _finish():
        def unpack(bi, carry):
            for j in range(pairs):
                ns_ref[bi, 2 * j] = sbd_ref[bi, j, 0:RW_HEAD, 0:RW_HEAD]
                ns_ref[bi, 2 * j + 1] = pltpu.roll(sbd_ref[bi, j, RW_HEAD:ROWS, :], RW_HEAD, 1)[:, 0:RW_HEAD]
            return carry

        lax.fori_loop(0, bb, unpack, 0)


def _rwkv_branch(p_all, rw_col, sbuf, s0, ns_all, layer, P, *, t, bb, tt, chunk, name):
    b = p_all.shape[0] // t
    nt = t // tt
    nrw = sbuf.shape[-1]
    heads = s0.shape[2]
    dr = heads * RW_HEAD
    nl = nrw - 3 * dr
    rows = bb * tt
    xw = rw_col
    assert xw % LANES == 0 and xw < nrw <= 2 * xw and rw_col + 2 * xw <= p_all.shape[-1]
    assert dr % LANES == 0 and RW_HEAD % chunk == 0 and tt % chunk == 0 and rows % RW_HEAD == 0
    assert tt == chunk or bb == 1
    lw = lambda shape: pl.BlockSpec((None,) + shape, lambda i, j: (layer,) + (0,) * len(shape))
    kern = functools.partial(_rwkv_kernel, bb=bb, tt=tt, chunk=chunk, dr=dr)
    big = lambda: pltpu.VMEM((rows, dr), F32)
    args = (p_all, p_all, sbuf, s0, P["rw_mu_p"], P["rw_w0"], P["rw_w2_p"], P["rw_a0"], P["rw_a2_p"], P["rw_g2_p"],
            P["rw_kk"], P["rw_ka"], P["rw_rk"], P["rw_gn_g"], P["rw_gn_b"], P["seg"], P["segt"], ns_all)
    return pl.pallas_call(
        kern,
        grid=(b // bb, t // tt),
        in_specs=[pl.BlockSpec((rows, xw), lambda i, j: (i * nt + j, 1)),
                  pl.BlockSpec((rows, xw), lambda i, j: (i * nt + j, 2)),
                  pl.BlockSpec((None, bb, 1, nrw), lambda i, j: (layer, i, 0, 0)),
                  pl.BlockSpec((None, bb, heads, RW_HEAD, RW_HEAD), lambda i, j: (layer, i, 0, 0, 0)),
                  lw((1, nrw)), lw((1, dr)), lw((nl, dr)), lw((1, dr)), lw((nl, dr)), lw((nl, dr)),
                  lw((1, dr)), lw((1, dr)), lw((1, dr)), lw((1, dr)), lw((1, dr)),
                  pl.BlockSpec((dr, LANES), lambda i, j: (0, 0)),
                  pl.BlockSpec((LANES, dr), lambda i, j: (0, 0)),
                  pl.BlockSpec(memory_space=pl.ANY)],
        out_specs=[pl.BlockSpec((rows, dr), lambda i, j: (i * (t // tt) + j, 0)),
                   pl.BlockSpec((bb, 1, nrw), lambda i, j: (i, 0, 0)),
                   pl.BlockSpec((None, bb, heads, RW_HEAD, RW_HEAD), lambda i, j: (layer, i, 0, 0, 0))],
        out_shape=[jax.ShapeDtypeStruct((b * t, dr), BF16),
                   jax.ShapeDtypeStruct((b, 1, nrw), F32),
                   jax.ShapeDtypeStruct(ns_all.shape, F32)],
        input_output_aliases={len(args) - 1: 2},
        scratch_shapes=[pltpu.VMEM((bb, dr // LANES, ROWS, LANES), F32),
                        pltpu.VMEM((bb, 1, nrw), F32),
                        big(), big(), big(), big(), big(), big(), big(),
                        pltpu.VMEM((RW_HEAD, LANES), F32)],
        compiler_params=pltpu.CompilerParams(
            dimension_semantics=("arbitrary", "arbitrary"), vmem_limit_bytes=VMEM_LIMIT),
        name=name,
    )(*args)


def _trunk(x, mod, boff, states, P, cfg, tag):
    bsz, t_len, d = x.shape
    m = bsz * t_len
    depth = mod.shape[0]
    lru_conv, lru_h, rw_shift, rw_s, ffn_conv = states
    dl = lru_h.shape[-1]
    nrw = P["rw_mu_p"].shape[-1]
    nrw0 = rw_shift.shape[-1]
    outs = ([], [], [], [])
    tm, tn = cfg["tm"], cfg["tn"]
    x = x.reshape(m, d)
    ns_all = jnp.zeros(rw_s.shape, F32)
    lru_h4 = lru_h[:, :, None, :]
    sbuf = jnp.pad(rw_shift, ((0, 0), (0, 0), (0, nrw - nrw0)))[:, :, None, :]
    norm = functools.partial(_norm_mod, bb=cfg["nbb"], tt=cfg["ntt"])
    for l in range(depth):
        h = norm(x.reshape(bsz, t_len, d), P["norm_mix"], l, mod, 1, 0, boff, name=f"norm_mix_{tag}")
        in_proj = functools.partial(_in_proj, h, P["w_in_t"], l, n_lru=2 * dl, n_rw=nrw, n_gates=2 * d, tm=tm, tn=tn)
        p_all = in_proj(gates=False, name=f"in_proj_{tag}")
        p_gates = in_proj(gates=True, name=f"in_gates_{tag}")
        ga, n_lru_buf, n_lru_h = _lru_branch(p_all, lru_conv, lru_h4, l, P, t=t_len,
                                             bb=cfg["lbb"], tt=cfg["ltt"], name=f"lru_{tag}")
        o_rw, n_shift, ns_all = _rwkv_branch(p_all, 2 * dl, sbuf, rw_s, ns_all, l, P, t=t_len, bb=cfg["rbb"],
                                             tt=cfg["rtt"], chunk=cfg["chunk"], name=f"rwkv_{tag}")
        merged = _merge(ga, o_rw, P["w_pa"], P["w_pb"], p_gates, l, tm=tm, tn=tn, name=f"merge_{tag}")
        x = _matmul_resid(merged, P["w_o"], l, x, mod, 2, boff, t_len, tm=tm, tn=tn, name=f"o_{tag}")

        h2 = norm(x.reshape(bsz, t_len, d), P["norm_ffn"], l, mod, 4, 3, boff, name=f"norm_ffn_{tag}")
        act, n_buf_g, n_buf_v = _ffn_up(h2, P["w_up"], P["ffn_conv_w"], P["ffn_conv_b"], ffn_conv, l,
                                        t_len=t_len, tm=tm, tn=cfg["tn_up"], name=f"ffn_up_{tag}")
        x = _matmul_resid(act, P["w_down"], l, x, mod, 5, boff, t_len, tm=cfg["tm_down"], tn=cfg["tn_res"],
                          name=f"ffn_down_{tag}")

        n_ffn_buf = jnp.concatenate([n_buf_g, n_buf_v], axis=-1)
        for lst, ns in zip(outs, (n_lru_buf, n_lru_h[:, 0, :], n_shift[:, 0, :nrw0], n_ffn_buf)):
            lst.append(ns)
    y = _final_norm(x.reshape(bsz, t_len, d), P["norm_final"], bb=cfg["nbb"], tt=cfg["ntt"], name=f"norm_final_{tag}")
    st = [jnp.stack(lst, axis=0) for lst in outs]
    return y, (st[0], st[1], st[2], ns_all, st[3])


def kernel(x_prompt, x_sample, c_prompt, c_sample, state_lru_conv, state_lru_h, state_rwkv_shift,
           state_rwkv_S, state_ffn_conv, w_ada, b_ada, norm_mix, norm_ffn, w_in, lru_conv_w,
           lru_conv_b, lru_wa, lru_ba, lru_wi, lru_bi, lru_lambda, w_pa, rw_mu, rw_w0, rw_w2, rw_a0,
           rw_a2, rw_g2, rw_kk, rw_ka, rw_rk, rw_gn_g, rw_gn_b, w_pb, w_o, w_up, ffn_conv_w,
           ffn_conv_b, w_down, norm_final):
    depth, d, _ = w_ada.shape
    bp, tp, _ = x_prompt.shape
    bs, ts, _ = x_sample.shape
    dl = lru_lambda.shape[-1]
    nblk, blk = lru_wa.shape[1], lru_wa.shape[2]
    heads = rw_rk.shape[1]
    dr = heads * RW_HEAD
    nrw0 = rw_mu.shape[-1]
    nl = _cdiv(nrw0 - 3 * dr, LANES) * LANES
    nrw = 3 * dr + nl
    lw_n, la_n = rw_w2.shape[1], rw_a2.shape[1]

    row = lambda p: p[:, None, :]
    per = LANES // blk
    eye = jnp.eye(per, dtype=F32)
    block_diag = lambda w: (eye[:, None, :, None] * w.reshape(depth, nblk // per, per, blk, 1, blk)
                            ).reshape(depth, nblk // per, LANES, LANES).astype(BF16)
    pad_rows = lambda w, off: jnp.pad(w, ((0, 0), (off, nl - off - w.shape[1]), (0, 0))).astype(BF16)
    head_of = jnp.arange(dr) // RW_HEAD
    seg = (head_of[:, None] == jnp.arange(LANES)[None, :]).astype(BF16)
    P = dict(
        norm_mix=row(norm_mix), norm_ffn=row(norm_ffn), norm_final=norm_final[None, :],
        w_in_t=jnp.swapaxes(w_in, 1, 2), lru_conv_w=lru_conv_w, lru_conv_b=row(lru_conv_b), lru_wa_bd=block_diag(lru_wa),
        lru_ba=row(lru_ba), lru_wi_bd=block_diag(lru_wi), lru_bi=row(lru_bi), lru_lambda=row(lru_lambda),
        w_pa=w_pa, w_pb=w_pb, w_o=w_o, w_up=w_up, w_down=w_down,
        rw_mu_p=row(jnp.pad(rw_mu, ((0, 0), (0, nrw - nrw0)))),
        rw_w0=row(rw_w0), rw_a0=row(rw_a0),
        rw_w2_p=pad_rows(rw_w2, 0), rw_a2_p=pad_rows(rw_a2, lw_n), rw_g2_p=pad_rows(rw_g2, lw_n + la_n),
        rw_kk=row(rw_kk), rw_ka=row(rw_ka), rw_rk=rw_rk.reshape(depth, 1, dr),
        rw_gn_g=row(rw_gn_g), rw_gn_b=row(rw_gn_b), seg=seg, segt=seg.T,
        ffn_conv_w=ffn_conv_w, ffn_conv_b=row(ffn_conv_b),
    )

    nb = bp + bs
    nb_pad = _cdiv(nb, SUBLANES) * SUBLANES
    c_all = jnp.concatenate([c_sample, c_prompt, jnp.zeros((nb_pad - nb, d), F32)], axis=0)
    mod = _adaln(c_all, w_ada, row(b_ada), tn=1024)[:, :, None, :]

    zeros = lambda *s: jnp.zeros((depth, bp) + s, F32)
    p_states = (zeros(lru_conv_w.shape[1] - 1, dl), zeros(dl), zeros(nrw0),
                zeros(heads, RW_HEAD, RW_HEAD), zeros(ffn_conv_w.shape[1] - 1, w_up.shape[-1]))
    s_states = (state_lru_conv, state_lru_h, state_rwkv_shift, state_rwkv_S, state_ffn_conv)

    tiles = dict(tm=1024, tn=1024, tn_res=512, tn_up=512, tm_down=512)
    cfg_p = dict(tiles, nbb=1, ntt=min(tp, 512), lbb=1, ltt=min(tp, 512),
                 rbb=1, rtt=min(tp, 256), chunk=min(tp, RW_HEAD))
    sb = min(bs, RW_HEAD // ts) if ts == SUBLANES else 1
    cfg_s = dict(tiles, nbb=min(bs, 64), ntt=ts, lbb=min(bs, 32), ltt=ts, rbb=sb, rtt=ts, chunk=ts)
    y_p, st_p = _trunk(x_prompt, mod, bs, p_states, P, cfg_p, "prompt")
    y_s, st_s = _trunk(x_sample, mod, 0, s_states, P, cfg_s, "sample")
    return (y_p, y_s) + st_p + st_s
```
